```python
import math
import jax, jax.numpy as jnp
from jax import lax
import numpy as np

D_MODEL = 2048
BATCH = 4
SEQ = 2048
DEPTH = 4

N_EVEN = (DEPTH + 1) // 2
N_ODD = DEPTH // 2

D_LRU = 1024
LRU_HEADS = 16
LRU_HEAD_DIM = D_LRU // LRU_HEADS
CONV_WIDTH = 4
LRU_C = 8.0

GLA_HEADS = 4
GLA_DK = 128
GLA_DV = 256
GLA_QK = GLA_HEADS * GLA_DK
GLA_V = GLA_HEADS * GLA_DV
GLA_RANK = 16
GLA_TAU = 16.0
GLA_CHUNK = 64

OFF_LRU_X = 0
OFF_LRU_G = OFF_LRU_X + D_LRU
OFF_Q = OFF_LRU_G + D_LRU
OFF_K = OFF_Q + GLA_QK
OFF_V = OFF_K + GLA_QK
OFF_OG = OFF_V + GLA_V
OFF_LR = OFF_OG + GLA_V
D_EVEN_IN = OFF_LR + GLA_RANK
D_EVEN_MIX = D_LRU + GLA_V

D_S5 = 1024
S5_GROUP = 16
S5_GROUPS = D_S5 // S5_GROUP
S5_STATE = 64
S5_MAX_RE = -1e-4

N_EXPERT_GROUPS = 4
EXPERTS_PER_GROUP = 8
N_EXPERTS = N_EXPERT_GROUPS * EXPERTS_PER_GROUP
D_EXPERT = 512
TOP_K_IN_GROUP = 2

D_PLE = 256

EPS = 1e-6

kernel_name = "hybrid_rglru_gla_s5_hmoe_trunk"


def rms_norm(x, g):
    xf = x.astype(jnp.float32)
    y = xf * lax.rsqrt(jnp.mean(xf * xf, axis=-1, keepdims=True) + EPS)
    return (y * g.astype(jnp.float32)).astype(x.dtype)


def causal_depthwise_conv(x, w, b):
    c = x.shape[-1]
    y = lax.conv_general_dilated(
        x, w[:, None, :].astype(x.dtype), window_strides=(1,),
        padding=[(CONV_WIDTH - 1, 0)], dimension_numbers=('NWC', 'WIO', 'NWC'),
        feature_group_count=c)
    return y + b


def linear_scan(a, b, axis):
    def combine(l, r):
        a_l, b_l = l
        a_r, b_r = r
        return a_r * a_l, a_r * b_l + b_r
    _, h = lax.associative_scan(combine, (a, b), axis=axis)
    return h


def rg_lru(x, w_a, b_a, w_x, b_x, lam):
    B, S, _ = x.shape
    xh = x.reshape(B, S, LRU_HEADS, LRU_HEAD_DIM)
    r = jax.nn.sigmoid(jnp.einsum('bshi,hij->bshj', xh, w_a).reshape(B, S, D_LRU) + b_a)
    i = jax.nn.sigmoid(jnp.einsum('bshi,hij->bshj', xh, w_x).reshape(B, S, D_LRU) + b_x)
    log_a = -LRU_C * r.astype(jnp.float32) * jax.nn.softplus(-lam.astype(jnp.float32))
    a = jnp.exp(log_a)
    mult = jnp.sqrt(-jnp.expm1(2.0 * log_a))
    u = mult * (i * x).astype(jnp.float32)
    h = linear_scan(a, u, axis=1)
    return h.astype(x.dtype)


def gla_chunked(q, k, v, gk):
    B, S, H, DK = q.shape
    DV = v.shape[-1]
    C = GLA_CHUNK
    n = S // C
    f32 = jnp.float32
    q = q.reshape(B, n, C, H, DK).astype(f32) * (DK ** -0.5)
    k = k.reshape(B, n, C, H, DK).astype(f32)
    v = v.reshape(B, n, C, H, DV).astype(f32)
    bcum = jnp.cumsum(gk.reshape(B, n, C, H, DK).astype(f32), axis=2)
    b_last = bcum[:, :, -1:]
    q_s = q * jnp.exp(bcum)
    k_s = k * jnp.exp(-bcum)
    k_end = k * jnp.exp(b_last - bcum)
    causal = jnp.tril(jnp.ones((C, C), dtype=bool))
    att = jnp.einsum('bnihk,bnjhk->bnhij', q_s, k_s)
    att = jnp.where(causal, att, 0.0)
    o = jnp.einsum('bnhij,bnjhv->bnihv', att, v)
    ds = jnp.einsum('bnjhk,bnjhv->bnhkv', k_end, v)
    decay = jnp.exp(b_last[:, :, 0])

    def step(s, inp):
        d, dsn = inp
        return d[..., None] * s + dsn, s

    s0 = jnp.zeros((B, H, DK, DV), f32)
    _, s_prev = lax.scan(step, s0, (jnp.moveaxis(decay, 1, 0), jnp.moveaxis(ds, 1, 0)))
    s_prev = jnp.moveaxis(s_prev, 0, 1)
    o = o + jnp.einsum('bnihk,bnhkv->bnihv', q_s, s_prev)
    return o.reshape(B, S, H, DV)


def even_mixer(xn, w_in, conv_w, conv_b, w_a, b_a, w_x, b_x, lam,
               w_gate_up, b_gate, head_norm, w_out):
    B, S, _ = xn.shape
    z = xn @ w_in
    x_lru = z[..., OFF_LRU_X:OFF_LRU_G]
    g_lru = z[..., OFF_LRU_G:OFF_Q]
    q = z[..., OFF_Q:OFF_K]
    k = z[..., OFF_K:OFF_V]
    v = z[..., OFF_V:OFF_OG]
    og = z[..., OFF_OG:OFF_LR]
    lr = z[..., OFF_LR:D_EVEN_IN]
    x_lru = causal_depthwise_conv(x_lru, conv_w, conv_b)
    y_lru = rg_lru(x_lru, w_a, b_a, w_x, b_x, lam) * jax.nn.gelu(g_lru)
    gk = jax.nn.log_sigmoid((lr @ w_gate_up + b_gate).astype(jnp.float32)) / GLA_TAU
    o = gla_chunked(q.reshape(B, S, GLA_HEADS, GLA_DK), k.reshape(B, S, GLA_HEADS, GLA_DK),
                    v.reshape(B, S, GLA_HEADS, GLA_DV), gk.reshape(B, S, GLA_HEADS, GLA_DK))
    o = o * lax.rsqrt(jnp.mean(o * o, axis=-1, keepdims=True) + EPS)
    y_gla = (o.reshape(B, S, GLA_V) * head_norm.astype(jnp.float32)).astype(xn.dtype) * jax.nn.silu(og)
    return jnp.concatenate([y_lru, y_gla], axis=-1) @ w_out


def s5_ssm(u, lam_re, lam_im, log_step, b_re, b_im, c_re, c_im, d_skip):
    B, S, _ = u.shape
    f32 = jnp.float32
    lr = jnp.minimum(lam_re.astype(f32), S5_MAX_RE)
    li = lam_im.astype(f32)
    dt = jnp.exp(log_step.astype(f32))[:, None]
    mag = jnp.exp(lr * dt)
    ab_re = mag * jnp.cos(li * dt)
    ab_im = mag * jnp.sin(li * dt)
    den = lr * lr + li * li
    coef_re = ((ab_re - 1.0) * lr + ab_im * li) / den
    coef_im = (ab_im * lr - (ab_re - 1.0) * li) / den
    bre = b_re.astype(f32)
    bim = b_im.astype(f32)
    bb_re = coef_re[..., None] * bre - coef_im[..., None] * bim
    bb_im = coef_re[..., None] * bim + coef_im[..., None] * bre
    ug = u.reshape(B, S, S5_GROUPS, S5_GROUP).astype(f32)
    bu_re = jnp.einsum('bsgh,gph->sbgp', ug, bb_re)
    bu_im = jnp.einsum('bsgh,gph->sbgp', ug, bb_im)
    a_re = jnp.broadcast_to(ab_re[None, None], (S, 1, S5_GROUPS, S5_STATE))
    a_im = jnp.broadcast_to(ab_im[None, None], (S, 1, S5_GROUPS, S5_STATE))

    def combine(l, r):
        ar_l, ai_l, br_l, bi_l = l
        ar_r, ai_r, br_r, bi_r = r
        return (ar_r * ar_l - ai_r * ai_l,
                ar_r * ai_l + ai_r * ar_l,
                ar_r * br_l - ai_r * bi_l + br_r,
                ar_r * bi_l + ai_r * br_l + bi_r)

    _, _, h_re, h_im = lax.associative_scan(combine, (a_re, a_im, bu_re, bu_im), axis=0)
    y = (jnp.einsum('sbgp,ghp->bsgh', h_re, c_re.astype(f32))
         - jnp.einsum('sbgp,ghp->bsgh', h_im, c_im.astype(f32)))
    y = y.reshape(B, S, D_S5) + d_skip.astype(f32) * u.astype(f32)
    return y.astype(u.dtype)


def odd_mixer(xn, w_in, lam_re, lam_im, log_step, b_re, b_im, c_re, c_im, d_skip, w_glu):
    u = xn @ w_in
    y = jax.nn.gelu(s5_ssm(u, lam_re, lam_im, log_step, b_re, b_im, c_re, c_im, d_skip))
    zz = y @ w_glu
    return zz[..., :D_MODEL] * jax.nn.sigmoid(zz[..., D_MODEL:])


def hier_moe(xn, w_rg, b_rg, w_re, b_re, w1, w3, w2):
    B, S, D = xn.shape
    f32 = jnp.float32
    t = xn.reshape(B * S, D)
    pg = jax.nn.softmax((t @ w_rg).astype(f32) + b_rg.astype(f32), axis=-1)
    gate_g, g_idx = lax.top_k(pg, 1)
    onehot_g = jax.nn.one_hot(g_idx[:, 0], N_EXPERT_GROUPS, dtype=f32)
    le = ((t @ w_re).astype(f32) + b_re.astype(f32)).reshape(-1, N_EXPERT_GROUPS, EXPERTS_PER_GROUP)
    le = jnp.einsum('tg,tge->te', onehot_g, le)
    pe = jax.nn.softmax(le, axis=-1)
    top_v, top_i = lax.top_k(pe, TOP_K_IN_GROUP)
    top_v = top_v / jnp.sum(top_v, axis=-1, keepdims=True)
    w_in_group = jnp.sum(jax.nn.one_hot(top_i, EXPERTS_PER_GROUP, dtype=f32) * top_v[..., None], axis=1)
    comb = (onehot_g[:, :, None] * (gate_g[:, :, None] * w_in_group[:, None, :])).astype(t.dtype)
    w1 = w1.reshape(N_EXPERT_GROUPS, EXPERTS_PER_GROUP, D, D_EXPERT)
    w3 = w3.reshape(N_EXPERT_GROUPS, EXPERTS_PER_GROUP, D, D_EXPERT)
    w2 = w2.reshape(N_EXPERT_GROUPS, EXPERTS_PER_GROUP, D_EXPERT, D)
    y = jnp.zeros_like(t)
    for g in range(N_EXPERT_GROUPS):
        h = jax.nn.silu(jnp.einsum('td,edf->tef', t, w1[g])) * jnp.einsum('td,edf->tef', t, w3[g])
        y = y + jnp.einsum('tef,efd->td', h * comb[:, g, :, None], w2[g])
    return y.reshape(B, S, D)


def setup_inputs(seed: int = 0) -> dict:
    key = jax.random.key(seed)
    ks = iter(jax.random.split(key, 48))
    f32 = jnp.float32

    def nrm(shape, scale):
        return jax.random.normal(next(ks), shape, f32) * scale

    def gain(shape):
        return 1.0 + nrm(shape, 0.01)

    inp = {}
    inp['x'] = nrm((BATCH, SEQ, D_MODEL), 1.0)
    inp['p'] = nrm((DEPTH, BATCH, SEQ, D_PLE), 1.0)
    inp['norm_mix'] = gain((DEPTH, D_MODEL))
    inp['norm_ffn'] = gain((DEPTH, D_MODEL))
    inp['norm_ple'] = gain((DEPTH, D_MODEL))
    inp['norm_final'] = gain((D_MODEL,))
    inp['ev_w_in'] = nrm((N_EVEN, D_MODEL, D_EVEN_IN), D_MODEL ** -0.5)
    inp['lru_conv_w'] = nrm((N_EVEN, CONV_WIDTH, D_LRU), CONV_WIDTH ** -0.5)
    inp['lru_conv_b'] = nrm((N_EVEN, D_LRU), 0.01)
    inp['lru_w_a'] = nrm((N_EVEN, LRU_HEADS, LRU_HEAD_DIM, LRU_HEAD_DIM), LRU_HEAD_DIM ** -0.5)
    inp['lru_b_a'] = nrm((N_EVEN, D_LRU), 0.01)
    inp['lru_w_x'] = nrm((N_EVEN, LRU_HEADS, LRU_HEAD_DIM, LRU_HEAD_DIM), LRU_HEAD_DIM ** -0.5)
    inp['lru_b_x'] = nrm((N_EVEN, D_LRU), 0.01)
    a8 = jax.random.uniform(next(ks), (N_EVEN, D_LRU), f32, minval=0.9, maxval=0.999)
    a = a8 ** (1.0 / LRU_C)
    inp['lru_lambda'] = jnp.log(a) - jnp.log1p(-a)
    inp['gla_w_gate'] = nrm((N_EVEN, GLA_RANK, GLA_QK), GLA_RANK ** -0.5)
    inp['gla_b_gate'] = nrm((N_EVEN, GLA_QK), 0.1)
    inp['gla_norm'] = gain((N_EVEN, GLA_V))
    inp['ev_w_out'] = nrm((N_EVEN, D_EVEN_MIX, D_MODEL), D_EVEN_MIX ** -0.5)
    inp['od_w_in'] = nrm((N_ODD, D_MODEL, D_S5), D_MODEL ** -0.5)
    inp['s5_lambda_re'] = -0.5 + nrm((N_ODD, S5_GROUPS, S5_STATE), 0.01)
    inp['s5_lambda_im'] = (jnp.pi * jnp.arange(S5_STATE, dtype=f32))[None, None, :] + nrm((N_ODD, S5_GROUPS, S5_STATE), 0.01)
    inp['s5_log_step'] = jax.random.uniform(next(ks), (N_ODD, S5_GROUPS), f32,
                                            minval=math.log(1e-3), maxval=math.log(1e-1))
    inp['s5_b_re'] = nrm((N_ODD, S5_GROUPS, S5_STATE, S5_GROUP), (2.0 * S5_GROUP) ** -0.5)
    inp['s5_b_im'] = nrm((N_ODD, S5_GROUPS, S5_STATE, S5_GROUP), (2.0 * S5_GROUP) ** -0.5)
    inp['s5_c_re'] = nrm((N_ODD, S5_GROUPS, S5_GROUP, S5_STATE), (2.0 * S5_STATE) ** -0.5)
    inp['s5_c_im'] = nrm((N_ODD, S5_GROUPS, S5_GROUP, S5_STATE), (2.0 * S5_STATE) ** -0.5)
    inp['s5_d'] = nrm((N_ODD, D_S5), 1.0)
    inp['od_w_glu'] = nrm((N_ODD, D_S5, 2 * D_MODEL), D_S5 ** -0.5)
    inp['moe_w_router_group'] = nrm((DEPTH, D_MODEL, N_EXPERT_GROUPS), D_MODEL ** -0.5)
    inp['moe_b_router_group'] = nrm((DEPTH, N_EXPERT_GROUPS), 0.01)
    inp['moe_w_router_expert'] = nrm((DEPTH, D_MODEL, N_EXPERTS), D_MODEL ** -0.5)
    inp['moe_b_router_expert'] = nrm((DEPTH, N_EXPERTS), 0.01)
    inp['moe_w1'] = nrm((DEPTH, N_EXPERTS, D_MODEL, D_EXPERT), D_MODEL ** -0.5)
    inp['moe_w3'] = nrm((DEPTH, N_EXPERTS, D_MODEL, D_EXPERT), D_MODEL ** -0.5)
    inp['moe_w2'] = nrm((DEPTH, N_EXPERTS, D_EXPERT, D_MODEL), D_EXPERT ** -0.5)
    inp['ple_w_gate'] = nrm((DEPTH, D_MODEL, D_MODEL), D_MODEL ** -0.5)
    inp['ple_b_gate'] = nrm((DEPTH, D_MODEL), 0.01)
    inp['ple_w_proj'] = nrm((DEPTH, D_PLE, D_MODEL), D_PLE ** -0.5)
    return inp


def reference(x, p, norm_mix, norm_ffn, norm_ple, norm_final,
              ev_w_in, lru_conv_w, lru_conv_b, lru_w_a, lru_b_a, lru_w_x, lru_b_x, lru_lambda,
              gla_w_gate, gla_b_gate, gla_norm, ev_w_out,
              od_w_in, s5_lambda_re, s5_lambda_im, s5_log_step, s5_b_re, s5_b_im,
              s5_c_re, s5_c_im, s5_d, od_w_glu,
              moe_w_router_group, moe_b_router_group, moe_w_router_expert, moe_b_router_expert,
              moe_w1, moe_w3, moe_w2,
              ple_w_gate, ple_b_gate, ple_w_proj):
    for l in range(DEPTH):
        xn = rms_norm(x, norm_mix[l])
        if l % 2 == 0:
            e = l // 2
            x = x + even_mixer(xn, ev_w_in[e], lru_conv_w[e], lru_conv_b[e], lru_w_a[e], lru_b_a[e],
                               lru_w_x[e], lru_b_x[e], lru_lambda[e], gla_w_gate[e], gla_b_gate[e],
                               gla_norm[e], ev_w_out[e])
        else:
            o = l // 2
            x = x + odd_mixer(xn, od_w_in[o], s5_lambda_re[o], s5_lambda_im[o], s5_log_step[o],
                              s5_b_re[o], s5_b_im[o], s5_c_re[o], s5_c_im[o], s5_d[o], od_w_glu[o])
        x = x + hier_moe(rms_norm(x, norm_ffn[l]), moe_w_router_group[l], moe_b_router_group[l],
                         moe_w_router_expert[l], moe_b_router_expert[l],
                         moe_w1[l], moe_w3[l], moe_w2[l])
        gate = jax.nn.sigmoid(rms_norm(x, norm_ple[l]) @ ple_w_gate[l] + ple_b_gate[l])
        x = x + gate * (p[l] @ ple_w_proj[l])
    return rms_norm(x, norm_final)
```

```python
import functools
import math

import jax
import jax.numpy as jnp
from jax import lax
from jax.experimental import pallas as pl
from jax.experimental.pallas import tpu as pltpu

F32 = jnp.float32
BF16 = jnp.bfloat16

D_MODEL = 2048
D_LRU = 1024
LRU_HEADS = 16
LRU_HEAD_DIM = D_LRU // LRU_HEADS
CONV_WIDTH = 4
LRU_C = 8.0
GLA_HEADS = 4
GLA_DK = 128
GLA_DV = 256
GLA_QK = GLA_HEADS * GLA_DK
GLA_V = GLA_HEADS * GLA_DV
GLA_RANK = 16
GLA_TAU = 16.0
GLA_CHUNK = 64
OFF_LRU_X = 0
OFF_LRU_G = OFF_LRU_X + D_LRU
OFF_Q = OFF_LRU_G + D_LRU
OFF_K = OFF_Q + GLA_QK
OFF_V = OFF_K + GLA_QK
OFF_OG = OFF_V + GLA_V
OFF_LR = OFF_OG + GLA_V
D_EVEN_Z = OFF_LR + GLA_QK
D_S5 = 1024
S5_GROUP = 16
S5_GROUPS = D_S5 // S5_GROUP
S5_STATE = 64
S5_MAX_RE = -1e-4
N_EXPERT_GROUPS = 4
EXPERTS_PER_GROUP = 8
N_EXPERTS = N_EXPERT_GROUPS * EXPERTS_PER_GROUP
D_EXPERT = 512
D_PLE = 256
EPS = 1e-6

LANES = 128
SUBLANES = 8
VMEM_LIMIT = 56 * 1024 * 1024

TM_MM = 1024
TN_MM = 512
TS_LRU = 256
TC_GLA = 256
TS_S5 = 256
S5_CB = 256
TM_ROUTE = 512
TM_EXP = 256
TM_PLE = 512
TN_PLE = 512
SCAN_UNROLL = 8
DMA_UNROLL = 8


def _cparams(sem):
    return pltpu.CompilerParams(dimension_semantics=sem, vmem_limit_bytes=VMEM_LIMIT)


def _rms(x, g):
    return x * lax.rsqrt(jnp.mean(x * x, axis=-1, keepdims=True) + EPS) * g


def _gelu_tanh(x):
    c = math.sqrt(2.0 / math.pi)
    return 0.5 * x * (1.0 + jnp.tanh(c * (x + 0.044715 * (x * x * x))))


def _softplus(x):
    return jnp.maximum(x, 0.0) + jnp.log1p(jnp.exp(-jnp.abs(x)))


def _silu(x):
    return x * jax.nn.sigmoid(x)


def _norm_matmul_kernel(x_ref, g_ref, w_ref, o_ref, xn_ref):
    @pl.when(pl.program_id(1) == 0)
    def _():
        xn_ref[...] = _rms(x_ref[...], g_ref[...]).astype(BF16)

    o_ref[...] = jnp.dot(xn_ref[...], w_ref[...], preferred_element_type=F32).astype(o_ref.dtype)


def _norm_matmul(x, g, w, out_dtype):
    t, d = x.shape
    n = w.shape[1]
    tm, tn = min(TM_MM, t), TN_MM
    return pl.pallas_call(
        _norm_matmul_kernel,
        grid=(t // tm, n // tn),
        in_specs=[pl.BlockSpec((tm, d), lambda i, j: (i, 0)),
                  pl.BlockSpec((1, d), lambda i, j: (0, 0)),
                  pl.BlockSpec((d, tn), lambda i, j: (0, j))],
        out_specs=pl.BlockSpec((tm, tn), lambda i, j: (i, j)),
        out_shape=jax.ShapeDtypeStruct((t, n), out_dtype),
        scratch_shapes=[pltpu.VMEM((tm, d), BF16)],
        compiler_params=_cparams(("parallel", "arbitrary")),
        name="norm_matmul",
    )(x, g, w)


def _mm2_res_kernel(a1_ref, a2_ref, w_ref, x_ref, o_ref):
    k1 = a1_ref.shape[1]
    acc = jnp.dot(a1_ref[...], w_ref[0:k1, :], preferred_element_type=F32)
    acc = acc + jnp.dot(a2_ref[...], w_ref[k1:, :], preferred_element_type=F32)
    o_ref[...] = x_ref[...] + acc


def _mm2_res(a1, a2, w, x):
    t, k1 = a1.shape
    k2 = a2.shape[1]
    n = w.shape[1]
    tm, tn = min(TM_MM, t), TN_MM
    return pl.pallas_call(
        _mm2_res_kernel,
        grid=(t // tm, n // tn),
        in_specs=[pl.BlockSpec((tm, k1), lambda i, j: (i, 0)),
                  pl.BlockSpec((tm, k2), lambda i, j: (i, 0)),
                  pl.BlockSpec((k1 + k2, tn), lambda i, j: (0, j)),
                  pl.BlockSpec((tm, tn), lambda i, j: (i, j))],
        out_specs=pl.BlockSpec((tm, tn), lambda i, j: (i, j)),
        out_shape=jax.ShapeDtypeStruct((t, n), F32),
        compiler_params=_cparams(("parallel", "arbitrary")),
        name="out_proj",
    )(a1, a2, w, x)


def _glu_res_kernel(y_ref, w1_ref, w2_ref, x_ref, o_ref):
    y = y_ref[...]
    z1 = jnp.dot(y, w1_ref[...], preferred_element_type=F32)
    z2 = jnp.dot(y, w2_ref[...], preferred_element_type=F32)
    o_ref[...] = x_ref[...] + z1 * jax.nn.sigmoid(z2)


def _glu_res(y, w, x):
    t, k = y.shape
    n = w.shape[1] // 2
    tm, tn = min(TM_MM, t), TN_MM
    nj = n // tn
    return pl.pallas_call(
        _glu_res_kernel,
        grid=(t // tm, nj),
        in_specs=[pl.BlockSpec((tm, k), lambda i, j: (i, 0)),
                  pl.BlockSpec((k, tn), lambda i, j: (0, j)),
                  pl.BlockSpec((k, tn), lambda i, j: (0, j + nj)),
                  pl.BlockSpec((tm, tn), lambda i, j: (i, j))],
        out_specs=pl.BlockSpec((tm, tn), lambda i, j: (i, j)),
        out_shape=jax.ShapeDtypeStruct((t, n), F32),
        compiler_params=_cparams(("parallel", "arbitrary")),
        name="glu_proj",
    )(y, w, w, x)


def _small_matmul_kernel(a_ref, b_ref, o_ref):
    o_ref[...] = jnp.dot(a_ref[...], b_ref[...], precision=lax.Precision.HIGHEST,
                         preferred_element_type=F32)


def _small_matmul(a, b):
    return pl.pallas_call(
        _small_matmul_kernel,
        out_shape=jax.ShapeDtypeStruct((a.shape[0], b.shape[1]), F32),
        name="small_matmul",
    )(a, b)


def _final_norm_kernel(x_ref, g_ref, o_ref):
    o_ref[...] = _rms(x_ref[...], g_ref[...])


def _final_norm(x, g):
    t, d = x.shape
    tm = min(TM_MM, t)
    return pl.pallas_call(
        _final_norm_kernel,
        grid=(t // tm,),
        in_specs=[pl.BlockSpec((tm, d), lambda i: (i, 0)),
                  pl.BlockSpec((1, d), lambda i: (0, 0))],
        out_specs=pl.BlockSpec((tm, d), lambda i: (i, 0)),
        out_shape=jax.ShapeDtypeStruct((t, d), F32),
        compiler_params=_cparams(("parallel",)),
        name="final_norm",
    )(x, g)


def _store_paired(dst_ref, v, rows):
    for c in range(v.shape[1] // LANES):
        dst_ref[c // 2, (c % 2) * rows:(c % 2 + 1) * rows, :] = v[:, c * LANES:(c + 1) * LANES]


def _load_paired(src_ref, rows, nchunks):
    return jnp.concatenate(
        [src_ref[c // 2, (c % 2) * rows:(c % 2 + 1) * rows, :] for c in range(nchunks)], axis=1)


def _pair_rows(v, batch):
    out = []
    for j in range(v.shape[1] // (2 * LANES)):
        lo = jnp.broadcast_to(v[:, (2 * j) * LANES:(2 * j + 1) * LANES], (batch, LANES))
        hi = jnp.broadcast_to(v[:, (2 * j + 1) * LANES:(2 * j + 2) * LANES], (batch, LANES))
        out.append(jnp.concatenate([lo, hi], axis=0))
    return out


def _lru_kernel(zx_ref, zg_ref, cw_ref, cb_ref, wa_ref, ba_ref, wx_ref, bx_ref, lam_ref,
                y_ref, xe_ref, a_ref, u_ref, h_ref):
    bsz, ts, c = zx_ref.shape
    rows = bsz * ts
    npair = c // (2 * LANES)
    nblk = wa_ref.shape[0]
    wblk = wa_ref.shape[1]

    @pl.when(pl.program_id(0) == 0)
    def _():
        xe_ref[:, 0:SUBLANES, :] = jnp.zeros((bsz, SUBLANES, c), F32)
        h_ref[...] = jnp.zeros(h_ref.shape, F32)

    x = zx_ref[...].astype(F32)
    xe_ref[:, SUBLANES:SUBLANES + ts, :] = x
    xc = cb_ref[...][None]
    for k in range(CONV_WIDTH):
        off = SUBLANES - (CONV_WIDTH - 1) + k
        xc = xc + cw_ref[k:k + 1, :][None] * xe_ref[:, off:off + ts, :]
    xe_ref[:, 0:SUBLANES, :] = x[:, ts - SUBLANES:, :]
    xc = xc.reshape(rows, c)

    xcb = xc.astype(BF16)
    ra, ia = [], []
    for b in range(nblk):
        xs = xcb[:, b * wblk:(b + 1) * wblk]
        ra.append(jnp.dot(xs, wa_ref[b], preferred_element_type=F32))
        ia.append(jnp.dot(xs, wx_ref[b], preferred_element_type=F32))
    r = jax.nn.sigmoid(jnp.concatenate(ra, axis=1) + ba_ref[...])
    i = jax.nn.sigmoid(jnp.concatenate(ia, axis=1) + bx_ref[...])
    log_a = (-LRU_C) * r * _softplus(-lam_ref[...])
    a = jnp.exp(log_a)
    mult = jnp.sqrt(1.0 - a * a)
    u = mult * (i * xc)
    _store_paired(a_ref, a, rows)
    _store_paired(u_ref, u, rows)

    def step(t0, hs):
        hs = list(hs)
        for k in range(SCAN_UNROLL):
            t = t0 * SCAN_UNROLL + k
            for j in range(npair):
                at = a_ref.at[j][pl.ds(t, 2 * bsz, stride=ts), :]
                ut = u_ref.at[j][pl.ds(t, 2 * bsz, stride=ts), :]
                hs[j] = at * hs[j] + ut
                u_ref.at[j][pl.ds(t, 2 * bsz, stride=ts), :] = hs[j]
        return tuple(hs)

    hs = lax.fori_loop(0, ts // SCAN_UNROLL, step, tuple(h_ref[j] for j in range(npair)))
    for j in range(npair):
        h_ref[j] = hs[j]

    h = _load_paired(u_ref, rows, c // LANES)
    g = zg_ref[...].astype(F32).reshape(rows, c)
    y_ref[...] = (h * _gelu_tanh(g)).astype(y_ref.dtype).reshape(bsz, ts, c)


def _lru(z3, conv_w, conv_b, wa_bd, ba, wx_bd, bx, lam):
    bsz, s, _ = z3.shape
    assert 2 * bsz == SUBLANES
    ts = min(TS_LRU, s)
    c = D_LRU
    rows = bsz * ts
    full = lambda shape: pl.BlockSpec(shape, lambda i: (0,) * len(shape))
    return pl.pallas_call(
        _lru_kernel,
        grid=(s // ts,),
        in_specs=[pl.BlockSpec((bsz, ts, c), lambda i: (0, i, OFF_LRU_X // c)),
                  pl.BlockSpec((bsz, ts, c), lambda i: (0, i, OFF_LRU_G // c)),
                  full(conv_w.shape), full(conv_b.shape), full(wa_bd.shape), full(ba.shape),
                  full(wx_bd.shape), full(bx.shape), full(lam.shape)],
        out_specs=pl.BlockSpec((bsz, ts, c), lambda i: (0, i, 0)),
        out_shape=jax.ShapeDtypeStruct((bsz, s, c), BF16),
        scratch_shapes=[pltpu.VMEM((bsz, ts + SUBLANES, c), F32),
                        pltpu.VMEM((c // (2 * LANES), 2 * rows, LANES), F32),
                        pltpu.VMEM((c // (2 * LANES), 2 * rows, LANES), F32),
                        pltpu.VMEM((c // (2 * LANES), SUBLANES, LANES), F32)],
        compiler_params=_cparams(("arbitrary",)),
        name="rg_lru",
    )(z3, z3, conv_w, conv_b, wa_bd, ba, wx_bd, bx, lam)


def _gla_kernel(q_ref, k_ref, v_ref, og_ref, gl_ref, bg_ref, hn_ref, y_ref, st_ref):
    tc = q_ref.shape[1]
    cs = GLA_CHUNK

    @pl.when(pl.program_id(2) == 0)
    def _():
        st_ref[...] = jnp.zeros(st_ref.shape, F32)

    row = lax.broadcasted_iota(jnp.int32, (cs, cs), 0)
    col = lax.broadcasted_iota(jnp.int32, (cs, cs), 1)
    causal = row >= col
    tri = causal.astype(F32)
    st = st_ref[...]
    for c in range(tc // cs):
        sl = slice(c * cs, (c + 1) * cs)
        x = gl_ref[0, sl, :].astype(F32) + bg_ref[...]
        gk = (jnp.minimum(x, 0.0) - jnp.log1p(jnp.exp(-jnp.abs(x)))) * (1.0 / GLA_TAU)
        bcum = jnp.dot(tri, gk, precision=lax.Precision.HIGHEST, preferred_element_type=F32)
        blast = bcum[cs - 1:cs, :]
        q = q_ref[0, sl, :].astype(F32) * (GLA_DK ** -0.5)
        k = k_ref[0, sl, :].astype(F32)
        v = v_ref[0, sl, :]
        qs = (q * jnp.exp(bcum)).astype(BF16)
        ks = (k * jnp.exp(-bcum)).astype(BF16)
        ke = (k * jnp.exp(blast - bcum)).astype(BF16)
        att = lax.dot_general(qs, ks, (((1,), (1,)), ((), ())), preferred_element_type=F32)
        att = jnp.where(causal, att, 0.0).astype(BF16)
        o = jnp.dot(att, v, preferred_element_type=F32)
        o = o + lax.dot_general(qs, st.astype(BF16), (((1,), (1,)), ((), ())),
                                preferred_element_type=F32)
        st = jnp.exp(blast) * st + lax.dot_general(v, ke, (((0,), (0,)), ((), ())),
                                                   preferred_element_type=F32)
        o = o * lax.rsqrt(jnp.mean(o * o, axis=-1, keepdims=True) + EPS)
        y = (o * hn_ref[...]) * _silu(og_ref[0, sl, :].astype(F32))
        y_ref[0, sl, :] = y.astype(y_ref.dtype)
    st_ref[...] = st


def _gla(z3, b_gate, head_norm):
    bsz, s, _ = z3.shape
    tc = min(TC_GLA, s)
    qb, kb, vb, ob, gb = (OFF_Q // GLA_DK, OFF_K // GLA_DK, OFF_V // GLA_DV, OFF_OG // GLA_DV,
                          OFF_LR // GLA_DK)
    return pl.pallas_call(
        _gla_kernel,
        grid=(bsz, GLA_HEADS, s // tc),
        in_specs=[pl.BlockSpec((1, tc, GLA_DK), lambda b, h, i: (b, i, qb + h)),
                  pl.BlockSpec((1, tc, GLA_DK), lambda b, h, i: (b, i, kb + h)),
                  pl.BlockSpec((1, tc, GLA_DV), lambda b, h, i: (b, i, vb + h)),
                  pl.BlockSpec((1, tc, GLA_DV), lambda b, h, i: (b, i, ob + h)),
                  pl.BlockSpec((1, tc, GLA_DK), lambda b, h, i: (b, i, gb + h)),
                  pl.BlockSpec((1, GLA_DK), lambda b, h, i: (0, h)),
                  pl.BlockSpec((1, GLA_DV), lambda b, h, i: (0, h))],
        out_specs=pl.BlockSpec((1, tc, GLA_DV), lambda b, h, i: (b, i, h)),
        out_shape=jax.ShapeDtypeStruct((bsz, s, GLA_V), BF16),
        scratch_shapes=[pltpu.VMEM((GLA_DV, GLA_DK), F32)],
        compiler_params=_cparams(("parallel", "parallel", "arbitrary")),
        name="gla",
    )(z3, z3, z3, z3, z3, b_gate, head_norm)


def _s5_kernel(u_ref, lre_ref, lim_ref, lst_ref, bre_ref, bim_ref, cre_ref, cim_ref, d_ref,
               y_ref, wbr_ref, wbi_ref, ar_ref, ai_ref, sr_ref, si_ref, hr_ref, hi_ref):
    bsz, ts, cb = u_ref.shape
    rows = bsz * ts
    nst = lre_ref.shape[2]
    npair = nst // (2 * LANES)

    @pl.when(pl.program_id(1) == 0)
    def _():
        lr = jnp.minimum(lre_ref[0], S5_MAX_RE)
        li = lim_ref[0]
        dt = jnp.exp(lst_ref[0])
        mag = jnp.exp(lr * dt)
        ab_re = mag * jnp.cos(li * dt)
        ab_im = mag * jnp.sin(li * dt)
        den = lr * lr + li * li
        coef_re = ((ab_re - 1.0) * lr + ab_im * li) / den
        coef_im = (ab_im * lr - (ab_re - 1.0) * li) / den
        bre = bre_ref[0]
        bim = bim_ref[0]
        wbr_ref[...] = (coef_re * bre - coef_im * bim).astype(BF16)
        wbi_ref[...] = (coef_re * bim + coef_im * bre).astype(BF16)
        for j, v in enumerate(_pair_rows(ab_re, bsz)):
            ar_ref[j] = v
        for j, v in enumerate(_pair_rows(ab_im, bsz)):
            ai_ref[j] = v
        hr_ref[...] = jnp.zeros(hr_ref.shape, F32)
        hi_ref[...] = jnp.zeros(hi_ref.shape, F32)

    u = u_ref[...].reshape(rows, cb)
    _store_paired(sr_ref, jnp.dot(u, wbr_ref[...], preferred_element_type=F32), rows)
    _store_paired(si_ref, jnp.dot(u, wbi_ref[...], preferred_element_type=F32), rows)

    ars = [ar_ref[j] for j in range(npair)]
    ais = [ai_ref[j] for j in range(npair)]

    def step(t0, carry):
        hr, hi = list(carry[0]), list(carry[1])
        for k in range(SCAN_UNROLL):
            t = t0 * SCAN_UNROLL + k
            for j in range(npair):
                br = sr_ref.at[j][pl.ds(t, 2 * bsz, stride=ts), :]
                bi = si_ref.at[j][pl.ds(t, 2 * bsz, stride=ts), :]
                nr = ars[j] * hr[j] - ais[j] * hi[j] + br
                ni = ars[j] * hi[j] + ais[j] * hr[j] + bi
                hr[j], hi[j] = nr, ni
                sr_ref.at[j][pl.ds(t, 2 * bsz, stride=ts), :] = nr
                si_ref.at[j][pl.ds(t, 2 * bsz, stride=ts), :] = ni
        return tuple(hr), tuple(hi)

    init = (tuple(hr_ref[j] for j in range(npair)), tuple(hi_ref[j] for j in range(npair)))
    hr, hi = lax.fori_loop(0, ts // SCAN_UNROLL, step, init)
    for j in range(npair):
        hr_ref[j] = hr[j]
        hi_ref[j] = hi[j]

    h_re = _load_paired(sr_ref, rows, nst // LANES).astype(BF16)
    h_im = _load_paired(si_ref, rows, nst // LANES).astype(BF16)
    y = jnp.dot(h_re, cre_ref[0], preferred_element_type=F32)
    y = y - jnp.dot(h_im, cim_ref[0], preferred_element_type=F32)
    y = y + d_ref[...] * u.astype(F32)
    y_ref[...] = _gelu_tanh(y).astype(y_ref.dtype).reshape(bsz, ts, cb)


def _s5(u3, lre, lim, lst, bre_bd, bim_bd, cre_bd, cim_bd, d_skip):
    bsz, s, c = u3.shape
    assert 2 * bsz == SUBLANES
    ts = min(TS_S5, s)
    cb = S5_CB
    ncb = c // cb
    nst = lre.shape[2]
    rows = bsz * ts
    npair = nst // (2 * LANES)
    blk = lambda shape: pl.BlockSpec((1,) + shape, lambda ci, i: (ci, 0, 0))
    return pl.pallas_call(
        _s5_kernel,
        grid=(ncb, s // ts),
        in_specs=[pl.BlockSpec((bsz, ts, cb), lambda ci, i: (0, i, ci)),
                  blk((1, nst)), blk((1, nst)), blk((1, nst)),
                  blk((cb, nst)), blk((cb, nst)), blk((nst, cb)), blk((nst, cb)),
                  pl.BlockSpec((1, cb), lambda ci, i: (0, ci))],
        out_specs=pl.BlockSpec((bsz, ts, cb), lambda ci, i: (0, i, ci)),
        out_shape=jax.ShapeDtypeStruct((bsz, s, c), BF16),
        scratch_shapes=[pltpu.VMEM((cb, nst), BF16), pltpu.VMEM((cb, nst), BF16),
                        pltpu.VMEM((npair, SUBLANES, LANES), F32),
                        pltpu.VMEM((npair, SUBLANES, LANES), F32),
                        pltpu.VMEM((npair, 2 * rows, LANES), F32),
                        pltpu.VMEM((npair, 2 * rows, LANES), F32),
                        pltpu.VMEM((npair, SUBLANES, LANES), F32),
                        pltpu.VMEM((npair, SUBLANES, LANES), F32)],
        compiler_params=_cparams(("parallel", "arbitrary")),
        name="s5",
    )(u3, lre, lim, lst, bre_bd, bim_bd, cre_bd, cim_bd, d_skip)


ROUTE_GROUP_LANE0 = 0
ROUTE_EXPERT_LANE0 = N_EXPERT_GROUPS
RI_E0, RI_E1, RI_C0, RI_C1, RI_R0, RI_R1 = 0, 1, 2, 3, 4, 5


def _router_kernel(x_ref, g_ref, w_ref, b_ref, xn_ref, ri_ref, cnt_ref, run_ref):
    tm = x_ref.shape[0]
    neg = -jnp.inf

    @pl.when(pl.program_id(0) == 0)
    def _():
        run_ref[...] = jnp.zeros(run_ref.shape, F32)

    xn = _rms(x_ref[...], g_ref[...])
    xn_ref[...] = xn
    logits = jnp.dot(xn, w_ref[...], precision=lax.Precision.HIGHEST,
                     preferred_element_type=F32) + b_ref[...]
    lane = lax.broadcasted_iota(jnp.int32, (tm, LANES), 1)

    def first_lane(mask):
        return jnp.min(jnp.where(mask, lane, LANES), axis=1, keepdims=True)

    lg = jnp.where(lane < N_EXPERT_GROUPS, logits, neg)
    mg = jnp.max(lg, axis=1, keepdims=True)
    sg = jnp.sum(jnp.exp(lg - mg), axis=1, keepdims=True)
    gate_g = 1.0 / sg
    g_idx = first_lane(lg == mg)
    lo = ROUTE_EXPERT_LANE0 + EXPERTS_PER_GROUP * g_idx
    in_group = jnp.abs(2 * (lane - lo) - (EXPERTS_PER_GROUP - 1)) < EXPERTS_PER_GROUP
    le = jnp.where(in_group, logits, neg)
    m1 = jnp.max(le, axis=1, keepdims=True)
    i1 = first_lane(le == m1)
    le2 = jnp.where(lane == i1, neg, le)
    m2 = jnp.max(le2, axis=1, keepdims=True)
    i2 = first_lane(le2 == m2)
    se = jnp.sum(jnp.exp(le - m1), axis=1, keepdims=True)
    p1 = 1.0 / se
    p2 = jnp.exp(m2 - m1) / se
    c0 = gate_g * (p1 / (p1 + p2))
    c1 = gate_g * (p2 / (p1 + p2))

    sel0 = lane == i1
    sel1 = lane == i2
    onehot = jnp.where(sel0, 1.0, jnp.where(sel1, 1.0, 0.0))
    r_i = lax.broadcasted_iota(jnp.int32, (tm, tm), 0)
    c_i = lax.broadcasted_iota(jnp.int32, (tm, tm), 1)
    before = jnp.where(c_i < r_i, 1.0, 0.0).astype(BF16)
    prefix = jnp.dot(before, onehot.astype(BF16), preferred_element_type=F32) + run_ref[...]
    rank0 = jnp.sum(jnp.where(sel0, prefix, 0.0), axis=1, keepdims=True)
    rank1 = jnp.sum(jnp.where(sel1, prefix, 0.0), axis=1, keepdims=True)
    run_ref[...] = run_ref[...] + jnp.sum(onehot, axis=0, keepdims=True)
    cnt_ref[...] = run_ref[...]

    e0 = (i1 - ROUTE_EXPERT_LANE0).astype(F32)
    e1 = (i2 - ROUTE_EXPERT_LANE0).astype(F32)
    rec = jnp.zeros((tm, LANES), F32)
    for ln, val in ((RI_E0, e0), (RI_E1, e1), (RI_C0, c0), (RI_C1, c1), (RI_R0, rank0),
                    (RI_R1, rank1)):
        rec = jnp.where(lane == ln, val, rec)
    ri_ref[...] = rec


def _router(x, g, w_route, b_route):
    t, d = x.shape
    tm = min(TM_ROUTE, t)
    return pl.pallas_call(
        _router_kernel,
        grid=(t // tm,),
        in_specs=[pl.BlockSpec((tm, d), lambda i: (i, 0)),
                  pl.BlockSpec((1, d), lambda i: (0, 0)),
                  pl.BlockSpec((d, LANES), lambda i: (0, 0)),
                  pl.BlockSpec((1, LANES), lambda i: (0, 0))],
        out_specs=[pl.BlockSpec((tm, d), lambda i: (i, 0)),
                   pl.BlockSpec((tm, LANES), lambda i: (i, 0)),
                   pl.BlockSpec((1, LANES), lambda i: (0, 0))],
        out_shape=[jax.ShapeDtypeStruct((t, d), F32),
                   jax.ShapeDtypeStruct((t, LANES), F32),
                   jax.ShapeDtypeStruct((1, LANES), F32)],
        scratch_shapes=[pltpu.VMEM((1, LANES), F32)],
        compiler_params=_cparams(("arbitrary",)),
        name="router",
    )(x, g, w_route, b_route)


def _experts_kernel(te_ref, nv_ref, code_ref, xn_hbm, w1_ref, w3_ref, w2_ref, y_hbm,
                    w1b_ref, w3b_ref, w2b_ref, xbuf, ybuf, gsem, ssem):
    i = pl.program_id(0)
    nt = pl.num_programs(0)
    tme = xbuf.shape[0]
    t_mask = xn_hbm.shape[0] - 1
    valid = i < nv_ref[0]
    base = i * tme

    def scatter_wait():
        pltpu.make_async_copy(ybuf, y_hbm.at[pl.ds(0, tme), :], ssem).wait()

    @pl.when(i == 0)
    def _():
        ybuf[...] = jnp.zeros(ybuf.shape, F32)
        fill = pltpu.make_async_copy(ybuf, y_hbm.at[pl.ds(y_hbm.shape[0] - tme, tme), :], ssem)
        fill.start()
        fill.wait()

    @pl.when(jnp.logical_and(i > 0, i - 1 < nv_ref[0]))
    def _():
        scatter_wait()

    @pl.when(valid)
    def _():
        def gather(r0, c):
            for k in range(DMA_UNROLL):
                r = r0 * DMA_UNROLL + k
                tok = code_ref[base + r] & t_mask
                pltpu.make_async_copy(xn_hbm.at[pl.ds(tok, 1), :], xbuf.at[pl.ds(r, 1), :],
                                      gsem).start()
            return c

        lax.fori_loop(0, tme // DMA_UNROLL, gather, 0)

        @pl.when(jnp.logical_or(i == 0, te_ref[i] != te_ref[jnp.maximum(i - 1, 0)]))
        def _():
            w1b_ref[...] = w1_ref[0].astype(BF16)
            w3b_ref[...] = w3_ref[0].astype(BF16)
            w2b_ref[...] = w2_ref[0].astype(BF16)

        pltpu.make_async_copy(xn_hbm.at[pl.ds(0, tme), :], xbuf, gsem).wait()
        xb = xbuf[...].astype(BF16)
        h = _silu(jnp.dot(xb, w1b_ref[...], preferred_element_type=F32))
        h = h * jnp.dot(xb, w3b_ref[...], preferred_element_type=F32)
        ybuf[...] = jnp.dot(h.astype(BF16), w2b_ref[...], preferred_element_type=F32)

        def scatter(r0, c):
            for k in range(DMA_UNROLL):
                r = r0 * DMA_UNROLL + k
                dst = code_ref[base + r]
                pltpu.make_async_copy(ybuf.at[pl.ds(r, 1), :], y_hbm.at[pl.ds(dst, 1), :],
                                      ssem).start()
            return c

        lax.fori_loop(0, tme // DMA_UNROLL, scatter, 0)

    @pl.when(jnp.logical_and(i == nt - 1, valid))
    def _():
        scatter_wait()


def _experts(tile_expert, n_valid, slot_code, xn, w1, w3, w2):
    t, d = xn.shape
    assert t & (t - 1) == 0
    e, _, f = w1.shape
    nt = tile_expert.shape[0]
    tme = slot_code.shape[0] // nt
    wmap = lambda i, te, nv, code: (te[i], 0, 0)
    return pl.pallas_call(
        _experts_kernel,
        grid_spec=pltpu.PrefetchScalarGridSpec(
            num_scalar_prefetch=3,
            grid=(nt,),
            in_specs=[pl.BlockSpec(memory_space=pl.ANY),
                      pl.BlockSpec((1, d, f), wmap),
                      pl.BlockSpec((1, d, f), wmap),
                      pl.BlockSpec((1, f, d), wmap)],
            out_specs=pl.BlockSpec(memory_space=pl.ANY),
            scratch_shapes=[pltpu.VMEM((d, f), BF16), pltpu.VMEM((d, f), BF16),
                            pltpu.VMEM((f, d), BF16),
                            pltpu.VMEM((tme, d), F32), pltpu.VMEM((tme, d), F32),
                            pltpu.SemaphoreType.DMA(()), pltpu.SemaphoreType.DMA(())]),
        out_shape=jax.ShapeDtypeStruct((2 * t + tme, d), F32),
        compiler_params=_cparams(("arbitrary",)),
        name="experts",
    )(tile_expert, n_valid, slot_code, xn, w1, w3, w2)


def _route_plan(rinfo, counts, tme):
    t = rinfo.shape[0]
    nt = 2 * t // tme + N_EXPERTS
    e = rinfo[:, RI_E0:RI_E1 + 1].astype(jnp.int32)
    rank = rinfo[:, RI_R0:RI_R1 + 1].astype(jnp.int32)
    cnt = counts[0, ROUTE_EXPERT_LANE0:ROUTE_EXPERT_LANE0 + N_EXPERTS].astype(jnp.int32)
    padded = ((cnt + tme - 1) // tme) * tme
    ends = jnp.cumsum(padded)
    offs = ends - padded
    pos = offs[e] + rank
    n_valid = (ends[-1] // tme).astype(jnp.int32)
    starts = jnp.arange(nt, dtype=jnp.int32) * tme
    tile_expert = jnp.sum((starts[:, None] >= ends[None, :]).astype(jnp.int32), axis=1)
    last = jnp.take(tile_expert, jnp.maximum(n_valid - 1, 0))
    tile_expert = jnp.where(jnp.arange(nt) < n_valid, tile_expert, last)
    tile_expert = jnp.minimum(tile_expert, N_EXPERTS - 1).astype(jnp.int32)
    tok = jnp.arange(t, dtype=jnp.int32)
    dst = jnp.stack([tok, tok + t], axis=1)
    slot_code = 2 * t + (jnp.arange(nt * tme, dtype=jnp.int32) % tme)
    slot_code = slot_code.at[pos.reshape(-1)].set(dst.reshape(-1))
    return tile_expert, n_valid.reshape(1), slot_code


def _ple_kernel(x_ref, ya_ref, yb_ref, ri_ref, g_ref, wg_ref, bg_ref, p_ref, wp_ref, o_ref,
                x2_ref, xn_ref):
    tn = o_ref.shape[1]
    j = pl.program_id(1)

    @pl.when(j == 0)
    def _():
        ri = ri_ref[...]
        x2 = x_ref[...] + ri[:, RI_C0:RI_C0 + 1] * ya_ref[...] + ri[:, RI_C1:RI_C1 + 1] * yb_ref[...]
        for c in range(x2_ref.shape[0]):
            x2_ref[c] = x2[:, c * tn:(c + 1) * tn]
        xn_ref[...] = _rms(x2, g_ref[...]).astype(BF16)

    gate = jax.nn.sigmoid(jnp.dot(xn_ref[...], wg_ref[...], preferred_element_type=F32)
                          + bg_ref[...])
    proj = jnp.dot(p_ref[...].astype(BF16), wp_ref[...], preferred_element_type=F32)
    o_ref[...] = x2_ref[j] + gate * proj


def _ple(x, y2, rinfo, g, wg, bg, p, wp):
    t, d = x.shape
    tm, tn = min(TM_PLE, t), TN_PLE
    nb = t // tm
    dp = p.shape[1]
    return pl.pallas_call(
        _ple_kernel,
        grid=(nb, d // tn),
        in_specs=[pl.BlockSpec((tm, d), lambda i, j: (i, 0)),
                  pl.BlockSpec((tm, d), lambda i, j: (i, 0)),
                  pl.BlockSpec((tm, d), lambda i, j: (i + nb, 0)),
                  pl.BlockSpec((tm, LANES), lambda i, j: (i, 0)),
                  pl.BlockSpec((1, d), lambda i, j: (0, 0)),
                  pl.BlockSpec((d, tn), lambda i, j: (0, j)),
                  pl.BlockSpec((1, tn), lambda i, j: (0, j)),
                  pl.BlockSpec((tm, dp), lambda i, j: (i, 0)),
                  pl.BlockSpec((dp, tn), lambda i, j: (0, j))],
        out_specs=pl.BlockSpec((tm, tn), lambda i, j: (i, j)),
        out_shape=jax.ShapeDtypeStruct((t, d), F32),
        scratch_shapes=[pltpu.VMEM((d // tn, tm, tn), F32), pltpu.VMEM((tm, d), BF16)],
        compiler_params=_cparams(("parallel", "arbitrary")),
        name="ple",
    )(x, y2, y2, rinfo, g, wg, bg, p, wp)


def _moe_ple(x, g_ffn, w_rg, b_rg, w_re, b_re, w1, w3, w2, g_ple, wg, bg, p, wp):
    d = x.shape[1]
    pad = LANES - N_EXPERT_GROUPS - N_EXPERTS
    w_route = jnp.concatenate([w_rg, w_re, jnp.zeros((d, pad), F32)], axis=1)
    b_route = jnp.concatenate([b_rg, b_re, jnp.zeros((pad,), F32)])[None]
    xn, rinfo, counts = _router(x, g_ffn[None], w_route, b_route)
    tile_expert, n_valid, slot_code = _route_plan(rinfo, counts, TM_EXP)
    y2 = _experts(tile_expert, n_valid, slot_code, xn, w1, w3, w2)
    return _ple(x, y2, rinfo, g_ple[None], wg.astype(BF16), bg[None], p, wp.astype(BF16))


def _block_diag(w, nb):
    h, d, _ = w.shape
    w4 = w.reshape(h // nb, nb, d, d)
    out = jnp.einsum('cadk,ab->cadbk', w4, jnp.eye(nb, dtype=w.dtype))
    return out.reshape(h // nb, nb * d, nb * d)


def _even_layer(x, bsz, g, w_in, conv_w, conv_b, w_a, b_a, w_x, b_x, lam, w_gate, b_gate,
                head_norm, w_out):
    t = x.shape[0]
    w_gk = _small_matmul(w_in[:, OFF_LR:], w_gate)
    w_cat = jnp.concatenate([w_in[:, :OFF_LR], w_gk], axis=1).astype(BF16)
    z = _norm_matmul(x, g[None], w_cat, BF16)
    z3 = z.reshape(bsz, t // bsz, D_EVEN_Z)
    heads_per_blk = 2 * LANES // LRU_HEAD_DIM
    y_lru = _lru(z3, conv_w, conv_b[None], _block_diag(w_a, heads_per_blk).astype(BF16),
                 b_a[None], _block_diag(w_x, heads_per_blk).astype(BF16), b_x[None], lam[None])
    y_gla = _gla(z3, b_gate[None], head_norm[None])
    return _mm2_res(y_lru.reshape(t, D_LRU), y_gla.reshape(t, GLA_V), w_out.astype(BF16), x)


def _odd_layer(x, bsz, g, w_in, lam_re, lam_im, log_step, b_re, b_im, c_re, c_im, d_skip,
               w_glu):
    t = x.shape[0]
    u = _norm_matmul(x, g[None], w_in.astype(BF16), BF16)
    gpb = S5_CB // S5_GROUP
    ncb = S5_GROUPS // gpb
    nst = gpb * S5_STATE
    eye = jnp.eye(gpb, dtype=F32)
    lay = lambda a: a.reshape(ncb, 1, nst)
    lst = jnp.repeat(log_step, S5_STATE)
    bexp = lambda b: jnp.einsum('cgph,gk->cghkp', b.reshape(ncb, gpb, S5_STATE, S5_GROUP),
                                eye).reshape(ncb, S5_CB, nst)
    cexp = lambda c: jnp.einsum('cgop,gk->cgpko', c.reshape(ncb, gpb, S5_GROUP, S5_STATE),
                                eye).reshape(ncb, nst, S5_CB).astype(BF16)
    y = _s5(u.reshape(bsz, t // bsz, D_S5), lay(lam_re), lay(lam_im), lay(lst),
            bexp(b_re), bexp(b_im), cexp(c_re), cexp(c_im), d_skip[None])
    return _glu_res(y.reshape(t, D_S5), w_glu.astype(BF16), x)


def kernel(x, p, norm_mix, norm_ffn, norm_ple, norm_final, ev_w_in, lru_conv_w, lru_conv_b, lru_w_a, lru_b_a, lru_w_x, lru_b_x, lru_lambda, gla_w_gate, gla_b_gate, gla_norm, ev_w_out, od_w_in, s5_lambda_re, s5_lambda_im, s5_log_step, s5_b_re, s5_b_im, s5_c_re, s5_c_im, s5_d, od_w_glu, moe_w_router_group, moe_b_router_group, moe_w_router_expert, moe_b_router_expert, moe_w1, moe_w3, moe_w2, ple_w_gate, ple_b_gate, ple_w_proj):
    bsz, s, d = x.shape
    t = bsz * s
    depth = p.shape[0]
    h = x.reshape(t, d)
    for l in range(depth):
        if l % 2 == 0:
            e = l // 2
            h = _even_layer(h, bsz, norm_mix[l], ev_w_in[e], lru_conv_w[e], lru_conv_b[e],
                            lru_w_a[e], lru_b_a[e], lru_w_x[e], lru_b_x[e], lru_lambda[e],
                            gla_w_gate[e], gla_b_gate[e], gla_norm[e], ev_w_out[e])
        else:
            o = l // 2
            h = _odd_layer(h, bsz, norm_mix[l], od_w_in[o], s5_lambda_re[o], s5_lambda_im[o],
                           s5_log_step[o], s5_b_re[o], s5_b_im[o], s5_c_re[o], s5_c_im[o],
                           s5_d[o], od_w_glu[o])
        h = _moe_ple(h, norm_ffn[l], moe_w_router_group[l], moe_b_router_group[l],
                     moe_w_router_expert[l], moe_b_router_expert[l], moe_w1[l], moe_w3[l],
                     moe_w2[l], norm_ple[l], ple_w_gate[l], ple_b_gate[l],
                     p[l].reshape(t, p.shape[-1]), ple_w_proj[l])
    return _final_norm(h, norm_final[None]).reshape(bsz, s, d)
```

```python
import functools
import math

import jax
import jax.numpy as jnp
from jax import lax
from jax.experimental import pallas as pl
from jax.experimental.pallas import tpu as pltpu

F32 = jnp.float32
BF16 = jnp.bfloat16

D_MODEL = 2048
D_LRU = 1024
LRU_HEADS = 16
LRU_HEAD_DIM = D_LRU // LRU_HEADS
CONV_WIDTH = 4
LRU_C = 8.0
GLA_HEADS = 4
GLA_DK = 128
GLA_DV = 256
GLA_QK = GLA_HEADS * GLA_DK
GLA_V = GLA_HEADS * GLA_DV
GLA_RANK = 16
GLA_TAU = 16.0
GLA_CHUNK = 64
OFF_LRU_X = 0
OFF_LRU_G = OFF_LRU_X + D_LRU
OFF_Q = OFF_LRU_G + D_LRU
OFF_K = OFF_Q + GLA_QK
OFF_V = OFF_K + GLA_QK
OFF_OG = OFF_V + GLA_V
OFF_LR = OFF_OG + GLA_V
D_EVEN_Z = OFF_LR + GLA_QK
D_S5 = 1024
S5_GROUP = 16
S5_GROUPS = D_S5 // S5_GROUP
S5_STATE = 64
S5_MAX_RE = -1e-4
N_EXPERT_GROUPS = 4
EXPERTS_PER_GROUP = 8
N_EXPERTS = N_EXPERT_GROUPS * EXPERTS_PER_GROUP
D_EXPERT = 512
D_PLE = 256
EPS = 1e-6

LANES = 128
SUBLANES = 8
VMEM_LIMIT = 56 * 1024 * 1024

TM_MM = 1024
TN_MM = 512
TS_LRU = 256
TC_GLA = 256
TS_S5 = 256
S5_CB = 256
TM_ROUTE = 512
TM_EXP = 256
TM_PLE = 256
PLE_PIECES = 4
EXP_H_PIECES = 2
EXP_Y_PIECES = 8
SCAN_UNROLL = 8
DMA_UNROLL = 8


def _cparams(sem):
    return pltpu.CompilerParams(dimension_semantics=sem, vmem_limit_bytes=VMEM_LIMIT)


def _rms(x, g):
    return x * lax.rsqrt(jnp.mean(x * x, axis=-1, keepdims=True) + EPS) * g


def _gelu_tanh(x):
    c = math.sqrt(2.0 / math.pi)
    return 0.5 * x * (1.0 + jnp.tanh(c * (x + 0.044715 * (x * x * x))))


def _softplus(x):
    return jnp.maximum(x, 0.0) + jnp.log1p(jnp.exp(-jnp.abs(x)))


def _silu(x):
    return x * jax.nn.sigmoid(x)


def _norm_matmul_kernel(x_ref, g_ref, w_ref, o_ref, xn_ref):
    @pl.when(pl.program_id(1) == 0)
    def _():
        xn_ref[...] = _rms(x_ref[...], g_ref[...]).astype(BF16)

    o_ref[...] = jnp.dot(xn_ref[...], w_ref[...], preferred_element_type=F32).astype(o_ref.dtype)


def _norm_matmul(x, g, w, out_dtype):
    t, d = x.shape
    n = w.shape[1]
    tm, tn = min(TM_MM, t), TN_MM
    return pl.pallas_call(
        _norm_matmul_kernel,
        grid=(t // tm, n // tn),
        in_specs=[pl.BlockSpec((tm, d), lambda i, j: (i, 0)),
                  pl.BlockSpec((1, d), lambda i, j: (0, 0)),
                  pl.BlockSpec((d, tn), lambda i, j: (0, j))],
        out_specs=pl.BlockSpec((tm, tn), lambda i, j: (i, j)),
        out_shape=jax.ShapeDtypeStruct((t, n), out_dtype),
        scratch_shapes=[pltpu.VMEM((tm, d), BF16)],
        compiler_params=_cparams(("parallel", "arbitrary")),
        name="norm_matmul",
    )(x, g, w)


def _mm2_res_kernel(a1_ref, a2_ref, w_ref, x_ref, o_ref):
    k1 = a1_ref.shape[1]
    acc = jnp.dot(a1_ref[...], w_ref[0:k1, :], preferred_element_type=F32)
    acc = acc + jnp.dot(a2_ref[...], w_ref[k1:, :], preferred_element_type=F32)
    o_ref[...] = x_ref[...] + acc


def _mm2_res(a1, a2, w, x):
    t, k1 = a1.shape
    k2 = a2.shape[1]
    n = w.shape[1]
    tm, tn = min(TM_MM, t), TN_MM
    return pl.pallas_call(
        _mm2_res_kernel,
        grid=(t // tm, n // tn),
        in_specs=[pl.BlockSpec((tm, k1), lambda i, j: (i, 0)),
                  pl.BlockSpec((tm, k2), lambda i, j: (i, 0)),
                  pl.BlockSpec((k1 + k2, tn), lambda i, j: (0, j)),
                  pl.BlockSpec((tm, tn), lambda i, j: (i, j))],
        out_specs=pl.BlockSpec((tm, tn), lambda i, j: (i, j)),
        out_shape=jax.ShapeDtypeStruct((t, n), F32),
        compiler_params=_cparams(("parallel", "arbitrary")),
        name="out_proj",
    )(a1, a2, w, x)


def _glu_res_kernel(y_ref, w1_ref, w2_ref, x_ref, o_ref):
    y = y_ref[...]
    z1 = jnp.dot(y, w1_ref[...], preferred_element_type=F32)
    z2 = jnp.dot(y, w2_ref[...], preferred_element_type=F32)
    o_ref[...] = x_ref[...] + z1 * jax.nn.sigmoid(z2)


def _glu_res(y, w, x):
    t, k = y.shape
    n = w.shape[1] // 2
    tm, tn = min(TM_MM, t), TN_MM
    nj = n // tn
    return pl.pallas_call(
        _glu_res_kernel,
        grid=(t // tm, nj),
        in_specs=[pl.BlockSpec((tm, k), lambda i, j: (i, 0)),
                  pl.BlockSpec((k, tn), lambda i, j: (0, j)),
                  pl.BlockSpec((k, tn), lambda i, j: (0, j + nj)),
                  pl.BlockSpec((tm, tn), lambda i, j: (i, j))],
        out_specs=pl.BlockSpec((tm, tn), lambda i, j: (i, j)),
        out_shape=jax.ShapeDtypeStruct((t, n), F32),
        compiler_params=_cparams(("parallel", "arbitrary")),
        name="glu_proj",
    )(y, w, w, x)


def _small_matmul_kernel(a_ref, b_ref, o_ref):
    o_ref[...] = jnp.dot(a_ref[...], b_ref[...], precision=lax.Precision.HIGHEST,
                         preferred_element_type=F32)


def _small_matmul(a, b):
    return pl.pallas_call(
        _small_matmul_kernel,
        out_shape=jax.ShapeDtypeStruct((a.shape[0], b.shape[1]), F32),
        name="small_matmul",
    )(a, b)


def _final_norm_kernel(x_ref, g_ref, o_ref):
    o_ref[...] = _rms(x_ref[...], g_ref[...])


def _final_norm(x, g):
    t, d = x.shape
    tm = min(TM_MM, t)
    return pl.pallas_call(
        _final_norm_kernel,
        grid=(t // tm,),
        in_specs=[pl.BlockSpec((tm, d), lambda i: (i, 0)),
                  pl.BlockSpec((1, d), lambda i: (0, 0))],
        out_specs=pl.BlockSpec((tm, d), lambda i: (i, 0)),
        out_shape=jax.ShapeDtypeStruct((t, d), F32),
        compiler_params=_cparams(("parallel",)),
        name="final_norm",
    )(x, g)


def _store_paired(dst_ref, v, bsz, ts):
    for c in range(v.shape[1] // LANES):
        for b in range(bsz):
            dst_ref.at[c // 2][pl.ds((c % 2) * bsz + b, ts, stride=SUBLANES), :] = (
                v[b * ts:(b + 1) * ts, c * LANES:(c + 1) * LANES])


def _load_paired(src_ref, bsz, ts, nchunks):
    cols = []
    for c in range(nchunks):
        cols.append(jnp.concatenate(
            [src_ref.at[c // 2][pl.ds((c % 2) * bsz + b, ts, stride=SUBLANES), :]
             for b in range(bsz)], axis=0))
    return jnp.concatenate(cols, axis=1)


def _step_rows(t0, k):
    base = pl.multiple_of(t0 * (SCAN_UNROLL * SUBLANES), SCAN_UNROLL * SUBLANES)
    return pl.ds(base + k * SUBLANES, SUBLANES)


def _pair_rows(v, batch):
    out = []
    for j in range(v.shape[1] // (2 * LANES)):
        lo = jnp.broadcast_to(v[:, (2 * j) * LANES:(2 * j + 1) * LANES], (batch, LANES))
        hi = jnp.broadcast_to(v[:, (2 * j + 1) * LANES:(2 * j + 2) * LANES], (batch, LANES))
        out.append(jnp.concatenate([lo, hi], axis=0))
    return out


def _lru_kernel(zx_ref, zg_ref, cw_ref, cb_ref, wa_ref, ba_ref, wx_ref, bx_ref, lam_ref,
                y_ref, xe_ref, a_ref, u_ref, h_ref):
    bsz, ts, c = zx_ref.shape
    rows = bsz * ts
    npair = c // (2 * LANES)
    nblk = wa_ref.shape[0]
    wblk = wa_ref.shape[1]

    @pl.when(pl.program_id(0) == 0)
    def _():
        xe_ref[:, 0:SUBLANES, :] = jnp.zeros((bsz, SUBLANES, c), F32)
        h_ref[...] = jnp.zeros(h_ref.shape, F32)

    x = zx_ref[...].astype(F32)
    xe_ref[:, SUBLANES:SUBLANES + ts, :] = x
    xc = cb_ref[...][None]
    for k in range(CONV_WIDTH):
        off = SUBLANES - (CONV_WIDTH - 1) + k
        xc = xc + cw_ref[k:k + 1, :][None] * xe_ref[:, off:off + ts, :]
    xe_ref[:, 0:SUBLANES, :] = x[:, ts - SUBLANES:, :]
    xc = xc.reshape(rows, c)

    xcb = xc.astype(BF16)
    ra, ia = [], []
    for b in range(nblk):
        xs = xcb[:, b * wblk:(b + 1) * wblk]
        ra.append(jnp.dot(xs, wa_ref[b], preferred_element_type=F32))
        ia.append(jnp.dot(xs, wx_ref[b], preferred_element_type=F32))
    r = jax.nn.sigmoid(jnp.concatenate(ra, axis=1) + ba_ref[...])
    i = jax.nn.sigmoid(jnp.concatenate(ia, axis=1) + bx_ref[...])
    log_a = (-LRU_C) * r * _softplus(-lam_ref[...])
    a = jnp.exp(log_a)
    mult = jnp.sqrt(1.0 - a * a)
    u = mult * (i * xc)
    _store_paired(a_ref, a, bsz, ts)
    _store_paired(u_ref, u, bsz, ts)

    def step(t0, hs):
        hs = list(hs)
        for k in range(SCAN_UNROLL):
            rs = _step_rows(t0, k)
            for j in range(npair):
                hs[j] = a_ref[j, rs, :] * hs[j] + u_ref[j, rs, :]
                u_ref[j, rs, :] = hs[j]
        return tuple(hs)

    hs = lax.fori_loop(0, ts // SCAN_UNROLL, step, tuple(h_ref[j] for j in range(npair)))
    for j in range(npair):
        h_ref[j] = hs[j]

    h = _load_paired(u_ref, bsz, ts, c // LANES)
    g = zg_ref[...].astype(F32).reshape(rows, c)
    y_ref[...] = (h * _gelu_tanh(g)).astype(y_ref.dtype).reshape(bsz, ts, c)


def _lru(z3, conv_w, conv_b, wa_bd, ba, wx_bd, bx, lam):
    bsz, s, _ = z3.shape
    assert 2 * bsz == SUBLANES
    ts = min(TS_LRU, s)
    c = D_LRU
    rows = bsz * ts
    full = lambda shape: pl.BlockSpec(shape, lambda i: (0,) * len(shape))
    return pl.pallas_call(
        _lru_kernel,
        grid=(s // ts,),
        in_specs=[pl.BlockSpec((bsz, ts, c), lambda i: (0, i, OFF_LRU_X // c)),
                  pl.BlockSpec((bsz, ts, c), lambda i: (0, i, OFF_LRU_G // c)),
                  full(conv_w.shape), full(conv_b.shape), full(wa_bd.shape), full(ba.shape),
                  full(wx_bd.shape), full(bx.shape), full(lam.shape)],
        out_specs=pl.BlockSpec((bsz, ts, c), lambda i: (0, i, 0)),
        out_shape=jax.ShapeDtypeStruct((bsz, s, c), BF16),
        scratch_shapes=[pltpu.VMEM((bsz, ts + SUBLANES, c), F32),
                        pltpu.VMEM((c // (2 * LANES), ts * SUBLANES, LANES), F32),
                        pltpu.VMEM((c // (2 * LANES), ts * SUBLANES, LANES), F32),
                        pltpu.VMEM((c // (2 * LANES), SUBLANES, LANES), F32)],
        compiler_params=_cparams(("arbitrary",)),
        name="rg_lru",
    )(z3, z3, conv_w, conv_b, wa_bd, ba, wx_bd, bx, lam)


def _gla_kernel(q_ref, k_ref, v_ref, og_ref, gl_ref, bg_ref, hn_ref, y_ref, st_ref):
    tc = q_ref.shape[1]
    cs = GLA_CHUNK

    @pl.when(pl.program_id(1) == 0)
    def _():
        st_ref[...] = jnp.zeros(st_ref.shape, F32)

    row = lax.broadcasted_iota(jnp.int32, (tc, tc), 0)
    col = lax.broadcasted_iota(jnp.int32, (tc, tc), 1)
    blk = jnp.where(row // cs == col // cs, 1.0, 0.0)
    tri = jnp.where(row >= col, blk, 0.0)
    causal = tri > 0.0
    for h in range(GLA_HEADS):
        dk = slice(h * GLA_DK, (h + 1) * GLA_DK)
        dv = slice(h * GLA_DV, (h + 1) * GLA_DV)
        x = gl_ref[0, :, dk].astype(F32) + bg_ref[:, dk]
        gk = (jnp.minimum(x, 0.0) - jnp.log1p(jnp.exp(-jnp.abs(x)))) * (1.0 / GLA_TAU)
        bcum = jnp.dot(tri, gk, precision=lax.Precision.HIGHEST, preferred_element_type=F32)
        btot = jnp.dot(blk, gk, precision=lax.Precision.HIGHEST, preferred_element_type=F32)
        q = q_ref[0, :, dk].astype(F32) * (GLA_DK ** -0.5)
        k = k_ref[0, :, dk].astype(F32)
        v = v_ref[0, :, dv]
        qs = (q * jnp.exp(bcum)).astype(BF16)
        ks = (k * jnp.exp(-bcum)).astype(BF16)
        ke = (k * jnp.exp(btot - bcum)).astype(BF16)
        att = lax.dot_general(qs, ks, (((1,), (1,)), ((), ())), preferred_element_type=F32)
        att = jnp.where(causal, att, 0.0).astype(BF16)
        o_intra = jnp.dot(att, v, preferred_element_type=F32)
        st = st_ref[h]
        outs = []
        for c in range(tc // cs):
            sl = slice(c * cs, (c + 1) * cs)
            outs.append(o_intra[sl] + lax.dot_general(
                qs[sl], st.astype(BF16), (((1,), (1,)), ((), ())), preferred_element_type=F32))
            ds = lax.dot_general(v[sl], ke[sl], (((0,), (0,)), ((), ())),
                                 preferred_element_type=F32)
            st = jnp.exp(btot[c * cs:c * cs + 1, :]) * st + ds
        st_ref[h] = st
        o = jnp.concatenate(outs, axis=0)
        o = o * lax.rsqrt(jnp.mean(o * o, axis=-1, keepdims=True) + EPS)
        y = (o * hn_ref[:, dv]) * _silu(og_ref[0, :, dv].astype(F32))
        y_ref[0, :, dv] = y.astype(y_ref.dtype)


def _gla(z3, b_gate, head_norm):
    bsz, s, _ = z3.shape
    tc = min(TC_GLA, s)
    qb, kb, vb, ob, gb = (OFF_Q // GLA_QK, OFF_K // GLA_QK, OFF_V // GLA_V, OFF_OG // GLA_V,
                          OFF_LR // GLA_QK)
    return pl.pallas_call(
        _gla_kernel,
        grid=(bsz, s // tc),
        in_specs=[pl.BlockSpec((1, tc, GLA_QK), lambda b, i: (b, i, qb)),
                  pl.BlockSpec((1, tc, GLA_QK), lambda b, i: (b, i, kb)),
                  pl.BlockSpec((1, tc, GLA_V), lambda b, i: (b, i, vb)),
                  pl.BlockSpec((1, tc, GLA_V), lambda b, i: (b, i, ob)),
                  pl.BlockSpec((1, tc, GLA_QK), lambda b, i: (b, i, gb)),
                  pl.BlockSpec((1, GLA_QK), lambda b, i: (0, 0)),
                  pl.BlockSpec((1, GLA_V), lambda b, i: (0, 0))],
        out_specs=pl.BlockSpec((1, tc, GLA_V), lambda b, i: (b, i, 0)),
        out_shape=jax.ShapeDtypeStruct((bsz, s, GLA_V), BF16),
        scratch_shapes=[pltpu.VMEM((GLA_HEADS, GLA_DV, GLA_DK), F32)],
        compiler_params=_cparams(("parallel", "arbitrary")),
        name="gla",
    )(z3, z3, z3, z3, z3, b_gate, head_norm)


def _s5_kernel(u_ref, lre_ref, lim_ref, lst_ref, bre_ref, bim_ref, cre_ref, cim_ref, d_ref,
               y_ref, wbr_ref, wbi_ref, ar_ref, ai_ref, sr_ref, si_ref, hr_ref, hi_ref):
    bsz, ts, cb = u_ref.shape
    rows = bsz * ts
    nst = lre_ref.shape[2]
    npair = nst // (2 * LANES)

    @pl.when(pl.program_id(1) == 0)
    def _():
        lr = jnp.minimum(lre_ref[0], S5_MAX_RE)
        li = lim_ref[0]
        dt = jnp.exp(lst_ref[0])
        mag = jnp.exp(lr * dt)
        ab_re = mag * jnp.cos(li * dt)
        ab_im = mag * jnp.sin(li * dt)
        den = lr * lr + li * li
        coef_re = ((ab_re - 1.0) * lr + ab_im * li) / den
        coef_im = (ab_im * lr - (ab_re - 1.0) * li) / den
        bre = bre_ref[0]
        bim = bim_ref[0]
        wbr_ref[...] = (coef_re * bre - coef_im * bim).astype(BF16)
        wbi_ref[...] = (coef_re * bim + coef_im * bre).astype(BF16)
        for j, v in enumerate(_pair_rows(ab_re, bsz)):
            ar_ref[j] = v
        for j, v in enumerate(_pair_rows(ab_im, bsz)):
            ai_ref[j] = v
        hr_ref[...] = jnp.zeros(hr_ref.shape, F32)
        hi_ref[...] = jnp.zeros(hi_ref.shape, F32)

    u = u_ref[...].reshape(rows, cb)
    _store_paired(sr_ref, jnp.dot(u, wbr_ref[...], preferred_element_type=F32), bsz, ts)
    _store_paired(si_ref, jnp.dot(u, wbi_ref[...], preferred_element_type=F32), bsz, ts)

    ars = [ar_ref[j] for j in range(npair)]
    ais = [ai_ref[j] for j in range(npair)]

    def step(t0, carry):
        hr, hi = list(carry[0]), list(carry[1])
        for k in range(SCAN_UNROLL):
            rs = _step_rows(t0, k)
            for j in range(npair):
                nr = ars[j] * hr[j] - ais[j] * hi[j] + sr_ref[j, rs, :]
                ni = ars[j] * hi[j] + ais[j] * hr[j] + si_ref[j, rs, :]
                hr[j], hi[j] = nr, ni
                sr_ref[j, rs, :] = nr
                si_ref[j, rs, :] = ni
        return tuple(hr), tuple(hi)

    init = (tuple(hr_ref[j] for j in range(npair)), tuple(hi_ref[j] for j in range(npair)))
    hr, hi = lax.fori_loop(0, ts // SCAN_UNROLL, step, init)
    for j in range(npair):
        hr_ref[j] = hr[j]
        hi_ref[j] = hi[j]

    h_re = _load_paired(sr_ref, bsz, ts, nst // LANES).astype(BF16)
    h_im = _load_paired(si_ref, bsz, ts, nst // LANES).astype(BF16)
    y = jnp.dot(h_re, cre_ref[0], preferred_element_type=F32)
    y = y - jnp.dot(h_im, cim_ref[0], preferred_element_type=F32)
    y = y + d_ref[...] * u.astype(F32)
    y_ref[...] = _gelu_tanh(y).astype(y_ref.dtype).reshape(bsz, ts, cb)


def _s5(u3, lre, lim, lst, bre_bd, bim_bd, cre_bd, cim_bd, d_skip):
    bsz, s, c = u3.shape
    assert 2 * bsz == SUBLANES
    ts = min(TS_S5, s)
    cb = S5_CB
    ncb = c // cb
    nst = lre.shape[2]
    rows = bsz * ts
    npair = nst // (2 * LANES)
    blk = lambda shape: pl.BlockSpec((1,) + shape, lambda ci, i: (ci, 0, 0))
    return pl.pallas_call(
        _s5_kernel,
        grid=(ncb, s // ts),
        in_specs=[pl.BlockSpec((bsz, ts, cb), lambda ci, i: (0, i, ci)),
                  blk((1, nst)), blk((1, nst)), blk((1, nst)),
                  blk((cb, nst)), blk((cb, nst)), blk((nst, cb)), blk((nst, cb)),
                  pl.BlockSpec((1, cb), lambda ci, i: (0, ci))],
        out_specs=pl.BlockSpec((bsz, ts, cb), lambda ci, i: (0, i, ci)),
        out_shape=jax.ShapeDtypeStruct((bsz, s, c), BF16),
        scratch_shapes=[pltpu.VMEM((cb, nst), BF16), pltpu.VMEM((cb, nst), BF16),
                        pltpu.VMEM((npair, SUBLANES, LANES), F32),
                        pltpu.VMEM((npair, SUBLANES, LANES), F32),
                        pltpu.VMEM((npair, ts * SUBLANES, LANES), F32),
                        pltpu.VMEM((npair, ts * SUBLANES, LANES), F32),
                        pltpu.VMEM((npair, SUBLANES, LANES), F32),
                        pltpu.VMEM((npair, SUBLANES, LANES), F32)],
        compiler_params=_cparams(("parallel", "arbitrary")),
        name="s5",
    )(u3, lre, lim, lst, bre_bd, bim_bd, cre_bd, cim_bd, d_skip)


ROUTE_GROUP_LANE0 = 0
ROUTE_EXPERT_LANE0 = N_EXPERT_GROUPS
RI_E0, RI_E1, RI_C0, RI_C1, RI_R0, RI_R1 = 0, 1, 2, 3, 4, 5


def _router_kernel(x_ref, g_ref, w_ref, b_ref, xn_ref, ri_ref, cnt_ref, run_ref):
    tm = x_ref.shape[0]
    neg = -jnp.inf

    @pl.when(pl.program_id(0) == 0)
    def _():
        run_ref[...] = jnp.zeros(run_ref.shape, F32)

    xn = _rms(x_ref[...], g_ref[...])
    xn_ref[...] = xn
    logits = jnp.dot(xn, w_ref[...], precision=lax.Precision.HIGHEST,
                     preferred_element_type=F32) + b_ref[...]
    lane = lax.broadcasted_iota(jnp.int32, (tm, LANES), 1)

    def first_lane(mask):
        return jnp.min(jnp.where(mask, lane, LANES), axis=1, keepdims=True)

    lg = jnp.where(lane < N_EXPERT_GROUPS, logits, neg)
    mg = jnp.max(lg, axis=1, keepdims=True)
    sg = jnp.sum(jnp.exp(lg - mg), axis=1, keepdims=True)
    gate_g = 1.0 / sg
    g_idx = first_lane(lg == mg)
    lo = ROUTE_EXPERT_LANE0 + EXPERTS_PER_GROUP * g_idx
    in_group = jnp.abs(2 * (lane - lo) - (EXPERTS_PER_GROUP - 1)) < EXPERTS_PER_GROUP
    le = jnp.where(in_group, logits, neg)
    m1 = jnp.max(le, axis=1, keepdims=True)
    i1 = first_lane(le == m1)
    le2 = jnp.where(lane == i1, neg, le)
    m2 = jnp.max(le2, axis=1, keepdims=True)
    i2 = first_lane(le2 == m2)
    se = jnp.sum(jnp.exp(le - m1), axis=1, keepdims=True)
    p1 = 1.0 / se
    p2 = jnp.exp(m2 - m1) / se
    c0 = gate_g * (p1 / (p1 + p2))
    c1 = gate_g * (p2 / (p1 + p2))

    sel0 = lane == i1
    sel1 = lane == i2
    onehot = jnp.where(sel0, 1.0, jnp.where(sel1, 1.0, 0.0))
    r_i = lax.broadcasted_iota(jnp.int32, (tm, tm), 0)
    c_i = lax.broadcasted_iota(jnp.int32, (tm, tm), 1)
    before = jnp.where(c_i < r_i, 1.0, 0.0).astype(BF16)
    prefix = jnp.dot(before, onehot.astype(BF16), preferred_element_type=F32) + run_ref[...]
    rank0 = jnp.sum(jnp.where(sel0, prefix, 0.0), axis=1, keepdims=True)
    rank1 = jnp.sum(jnp.where(sel1, prefix, 0.0), axis=1, keepdims=True)
    run_ref[...] = run_ref[...] + jnp.sum(onehot, axis=0, keepdims=True)
    cnt_ref[...] = run_ref[...]

    e0 = (i1 - ROUTE_EXPERT_LANE0).astype(F32)
    e1 = (i2 - ROUTE_EXPERT_LANE0).astype(F32)
    rec = jnp.zeros((tm, LANES), F32)
    for ln, val in ((RI_E0, e0), (RI_E1, e1), (RI_C0, c0), (RI_C1, c1), (RI_R0, rank0),
                    (RI_R1, rank1)):
        rec = jnp.where(lane == ln, val, rec)
    ri_ref[...] = rec


def _router(x, g, w_route, b_route):
    t, d = x.shape
    tm = min(TM_ROUTE, t)
    return pl.pallas_call(
        _router_kernel,
        grid=(t // tm,),
        in_specs=[pl.BlockSpec((tm, d), lambda i: (i, 0)),
                  pl.BlockSpec((1, d), lambda i: (0, 0)),
                  pl.BlockSpec((d, LANES), lambda i: (0, 0)),
                  pl.BlockSpec((1, LANES), lambda i: (0, 0))],
        out_specs=[pl.BlockSpec((tm, d), lambda i: (i, 0)),
                   pl.BlockSpec((tm, LANES), lambda i: (i, 0)),
                   pl.BlockSpec((1, LANES), lambda i: (0, 0))],
        out_shape=[jax.ShapeDtypeStruct((t, d), F32),
                   jax.ShapeDtypeStruct((t, LANES), F32),
                   jax.ShapeDtypeStruct((1, LANES), F32)],
        scratch_shapes=[pltpu.VMEM((1, LANES), F32)],
        compiler_params=_cparams(("arbitrary",)),
        name="router",
    )(x, g, w_route, b_route)


def _piece_groups(n_rows, weights):
    total = sum(weights)
    bounds = [round(n_rows * sum(weights[:k]) / total) for k in range(len(weights) + 1)]
    return [range(bounds[k], bounds[k + 1]) for k in range(len(weights))]


def _experts_kernel(te_ref, nv_ref, tok_ref, xn_hbm, w1_ref, w3_ref, w2_ref, y_ref,
                    w1b_ref, w3b_ref, w2b_ref, xbuf, hbuf, gsem):
    i = pl.program_id(0)
    nv = nv_ref[0]
    tme, d = y_ref.shape
    f = hbuf.shape[1]
    slot = i % 2
    other = 1 - slot

    def gather_row(tile, r, s):
        pltpu.make_async_copy(xn_hbm.at[pl.ds(tok_ref[tile * tme + r], 1), :],
                              xbuf.at[s, pl.ds(r, 1), :], gsem.at[s]).start()

    def gather_wait(s):
        pltpu.make_async_copy(xn_hbm.at[pl.ds(0, tme), :], xbuf.at[s], gsem.at[s]).wait()

    @pl.when(i == 0)
    def _():
        def first_gather(r0, c):
            for k in range(DMA_UNROLL):
                gather_row(0, r0 * DMA_UNROLL + k, 0)
            return c

        lax.fori_loop(0, tme // DMA_UNROLL, first_gather, 0)

    @pl.when(i >= nv)
    def _():
        y_ref[...] = jnp.zeros(y_ref.shape, F32)

    @pl.when(i < nv)
    def _():
        gather_wait(slot)

        @pl.when(jnp.logical_or(i == 0, te_ref[i] != te_ref[jnp.maximum(i - 1, 0)]))
        def _():
            w1b_ref[...] = w1_ref[0, 0].astype(BF16)
            w3b_ref[...] = w3_ref[0, 0].astype(BF16)
            w2b_ref[...] = w2_ref[0, 0].astype(BF16)

        nxt = jnp.minimum(i + 1, nv - 1)
        n_h, n_y = EXP_H_PIECES, EXP_Y_PIECES
        groups = _piece_groups(tme, [2 * n_y] * n_h + [n_h] * n_y)

        xb = xbuf[slot].astype(BF16)
        hw = f // n_h
        for c in range(n_h):
            cs = slice(c * hw, (c + 1) * hw)
            h1 = jnp.dot(xb, w1b_ref[:, cs], preferred_element_type=F32)
            h3 = jnp.dot(xb, w3b_ref[:, cs], preferred_element_type=F32)
            hbuf[:, cs] = (_silu(h1) * h3).astype(BF16)
            for r in groups[c]:
                gather_row(nxt, r, other)
        hb = hbuf[...]
        yw = d // n_y
        for c in range(n_y):
            cs = slice(c * yw, (c + 1) * yw)
            y_ref[:, cs] = jnp.dot(hb, w2b_ref[:, cs], preferred_element_type=F32)
            for r in groups[n_h + c]:
                gather_row(nxt, r, other)

        @pl.when(i == nv - 1)
        def _():
            gather_wait(other)


def _experts(tile_expert, n_valid, slot_tok, xn, w1, w3, w2, layer):
    t, d = xn.shape
    f = w1.shape[-1]
    nt = tile_expert.shape[0]
    tme = slot_tok.shape[0] // nt
    wmap = lambda i, te, nv, tok: (layer, te[i], 0, 0)
    return pl.pallas_call(
        _experts_kernel,
        grid_spec=pltpu.PrefetchScalarGridSpec(
            num_scalar_prefetch=3,
            grid=(nt,),
            in_specs=[pl.BlockSpec(memory_space=pl.ANY),
                      pl.BlockSpec((1, 1, d, f), wmap),
                      pl.BlockSpec((1, 1, d, f), wmap),
                      pl.BlockSpec((1, 1, f, d), wmap)],
            out_specs=pl.BlockSpec((tme, d), lambda i, te, nv, tok: (i, 0)),
            scratch_shapes=[pltpu.VMEM((d, f), BF16), pltpu.VMEM((d, f), BF16),
                            pltpu.VMEM((f, d), BF16),
                            pltpu.VMEM((2, tme, d), F32),
                            pltpu.VMEM((tme, f), BF16),
                            pltpu.SemaphoreType.DMA((2,))]),
        out_shape=jax.ShapeDtypeStruct((nt * tme, d), F32),
        compiler_params=_cparams(("arbitrary",)),
        name="experts",
    )(tile_expert, n_valid, slot_tok, xn, w1, w3, w2)


def _route_plan(rinfo, counts, tme):
    t = rinfo.shape[0]
    nt = 2 * t // tme + N_EXPERTS
    e = rinfo[:, RI_E0:RI_E1 + 1].astype(jnp.int32)
    rank = rinfo[:, RI_R0:RI_R1 + 1].astype(jnp.int32)
    cnt = counts[0, ROUTE_EXPERT_LANE0:ROUTE_EXPERT_LANE0 + N_EXPERTS].astype(jnp.int32)
    padded = ((cnt + tme - 1) // tme) * tme
    ends = jnp.cumsum(padded)
    offs = ends - padded
    pos = offs[e] + rank
    n_valid = (ends[-1] // tme).astype(jnp.int32)
    starts = jnp.arange(nt, dtype=jnp.int32) * tme
    tile_expert = jnp.sum((starts[:, None] >= ends[None, :]).astype(jnp.int32), axis=1)
    last = jnp.take(tile_expert, jnp.maximum(n_valid - 1, 0))
    tile_expert = jnp.where(jnp.arange(nt) < n_valid, tile_expert, last)
    tile_expert = jnp.minimum(tile_expert, N_EXPERTS - 1).astype(jnp.int32)
    tok = jnp.arange(t, dtype=jnp.int32)
    pos_flat = pos.T.reshape(-1)
    slot_tok = jnp.zeros((nt * tme,), jnp.int32).at[pos_flat].set(jnp.tile(tok, 2))
    return tile_expert, n_valid.reshape(1), slot_tok, pos_flat


def _ple_kernel(pos_ref, ys_hbm, x_ref, ri_ref, g_ref, wg_ref, bg_ref, p_ref, wp_ref, o_ref,
                ybuf, sem):
    i = pl.program_id(0)
    nb = pl.num_programs(0)
    tm, d = o_ref.shape
    t = nb * tm
    slot = i % 2
    other = 1 - slot

    def gather_row(blk, r, s):
        for k in range(2):
            src = pos_ref[k * t + blk * tm + r]
            pltpu.make_async_copy(ys_hbm.at[pl.ds(src, 1), :], ybuf.at[s, k, pl.ds(r, 1), :],
                                  sem.at[s]).start()

    def gather_wait(s):
        for k in range(2):
            pltpu.make_async_copy(ys_hbm.at[pl.ds(0, tm), :], ybuf.at[s, k], sem.at[s]).wait()

    @pl.when(i == 0)
    def _():
        def first_gather(r0, c):
            for k in range(DMA_UNROLL):
                gather_row(0, r0 * DMA_UNROLL + k, 0)
            return c

        lax.fori_loop(0, tm // DMA_UNROLL, first_gather, 0)

    gather_wait(slot)
    nxt = jnp.minimum(i + 1, nb - 1)
    groups = _piece_groups(tm, [1] * PLE_PIECES)
    ri = ri_ref[...]
    x2 = x_ref[...] + ri[:, RI_C0:RI_C0 + 1] * ybuf[slot, 0] + ri[:, RI_C1:RI_C1 + 1] * ybuf[slot, 1]
    o_ref[...] = x2
    xn = _rms(x2, g_ref[...]).astype(BF16)
    pb = p_ref[...].astype(BF16)
    w = d // PLE_PIECES
    for c in range(PLE_PIECES):
        cs = slice(c * w, (c + 1) * w)
        gate = jax.nn.sigmoid(jnp.dot(xn, wg_ref[:, cs], preferred_element_type=F32)
                              + bg_ref[:, cs])
        proj = jnp.dot(pb, wp_ref[:, cs], preferred_element_type=F32)
        o_ref[:, cs] = o_ref[:, cs] + gate * proj
        for r in groups[c]:
            gather_row(nxt, r, other)

    @pl.when(i == nb - 1)
    def _():
        gather_wait(other)


def _ple(x, ys, pos_flat, rinfo, g, wg, bg, p, wp):
    t, d = x.shape
    tm = min(TM_PLE, t)
    dp = p.shape[1]
    return pl.pallas_call(
        _ple_kernel,
        grid_spec=pltpu.PrefetchScalarGridSpec(
            num_scalar_prefetch=1,
            grid=(t // tm,),
            in_specs=[pl.BlockSpec(memory_space=pl.ANY),
                      pl.BlockSpec((tm, d), lambda i, pos: (i, 0)),
                      pl.BlockSpec((tm, LANES), lambda i, pos: (i, 0)),
                      pl.BlockSpec((1, d), lambda i, pos: (0, 0)),
                      pl.BlockSpec((d, d), lambda i, pos: (0, 0)),
                      pl.BlockSpec((1, d), lambda i, pos: (0, 0)),
                      pl.BlockSpec((tm, dp), lambda i, pos: (i, 0)),
                      pl.BlockSpec((dp, d), lambda i, pos: (0, 0))],
            out_specs=pl.BlockSpec((tm, d), lambda i, pos: (i, 0)),
            scratch_shapes=[pltpu.VMEM((2, 2, tm, d), F32), pltpu.SemaphoreType.DMA((2,))]),
        out_shape=jax.ShapeDtypeStruct((t, d), F32),
        compiler_params=_cparams(("arbitrary",)),
        name="ple",
    )(pos_flat, ys, x, rinfo, g, wg, bg, p, wp)


def _moe_ple(x, layer, g_ffn, w_rg, b_rg, w_re, b_re, w1, w3, w2, g_ple, wg, bg, p, wp):
    d = x.shape[1]
    pad = LANES - N_EXPERT_GROUPS - N_EXPERTS
    w_route = jnp.concatenate([w_rg, w_re, jnp.zeros((d, pad), F32)], axis=1)
    b_route = jnp.concatenate([b_rg, b_re, jnp.zeros((pad,), F32)])[None]
    xn, rinfo, counts = _router(x, g_ffn[None], w_route, b_route)
    tile_expert, n_valid, slot_tok, pos_flat = _route_plan(rinfo, counts, TM_EXP)
    ys = _experts(tile_expert, n_valid, slot_tok, xn, w1, w3, w2, layer)
    return _ple(x, ys, pos_flat, rinfo, g_ple[None], wg.astype(BF16), bg[None], p,
                wp.astype(BF16))


def _block_diag(w, nb):
    h, d, _ = w.shape
    w4 = w.reshape(h // nb, nb, d, d)
    out = jnp.einsum('cadk,ab->cadbk', w4, jnp.eye(nb, dtype=w.dtype))
    return out.reshape(h // nb, nb * d, nb * d)


def _even_layer(x, bsz, g, w_in, conv_w, conv_b, w_a, b_a, w_x, b_x, lam, w_gate, b_gate,
                head_norm, w_out):
    t = x.shape[0]
    w_gk = _small_matmul(w_in[:, OFF_LR:], w_gate)
    w_cat = jnp.concatenate([w_in[:, :OFF_LR], w_gk], axis=1).astype(BF16)
    z = _norm_matmul(x, g[None], w_cat, BF16)
    z3 = z.reshape(bsz, t // bsz, D_EVEN_Z)
    heads_per_blk = 2 * LANES // LRU_HEAD_DIM
    y_lru = _lru(z3, conv_w, conv_b[None], _block_diag(w_a, heads_per_blk).astype(BF16),
                 b_a[None], _block_diag(w_x, heads_per_blk).astype(BF16), b_x[None], lam[None])
    y_gla = _gla(z3, b_gate[None], head_norm[None])
    return _mm2_res(y_lru.reshape(t, D_LRU), y_gla.reshape(t, GLA_V), w_out.astype(BF16), x)


def _odd_layer(x, bsz, g, w_in, lam_re, lam_im, log_step, b_re, b_im, c_re, c_im, d_skip,
               w_glu):
    t = x.shape[0]
    u = _norm_matmul(x, g[None], w_in.astype(BF16), BF16)
    gpb = S5_CB // S5_GROUP
    ncb = S5_GROUPS // gpb
    nst = gpb * S5_STATE
    eye = jnp.eye(gpb, dtype=F32)
    lay = lambda a: a.reshape(ncb, 1, nst)
    lst = jnp.repeat(log_step, S5_STATE)
    bexp = lambda b: jnp.einsum('cgph,gk->cghkp', b.reshape(ncb, gpb, S5_STATE, S5_GROUP),
                                eye).reshape(ncb, S5_CB, nst)
    cexp = lambda c: jnp.einsum('cgop,gk->cgpko', c.reshape(ncb, gpb, S5_GROUP, S5_STATE),
                                eye).reshape(ncb, nst, S5_CB).astype(BF16)
    y = _s5(u.reshape(bsz, t // bsz, D_S5), lay(lam_re), lay(lam_im), lay(lst),
            bexp(b_re), bexp(b_im), cexp(c_re), cexp(c_im), d_skip[None])
    return _glu_res(y.reshape(t, D_S5), w_glu.astype(BF16), x)


def kernel(x, p, norm_mix, norm_ffn, norm_ple, norm_final, ev_w_in, lru_conv_w, lru_conv_b, lru_w_a, lru_b_a, lru_w_x, lru_b_x, lru_lambda, gla_w_gate, gla_b_gate, gla_norm, ev_w_out, od_w_in, s5_lambda_re, s5_lambda_im, s5_log_step, s5_b_re, s5_b_im, s5_c_re, s5_c_im, s5_d, od_w_glu, moe_w_router_group, moe_b_router_group, moe_w_router_expert, moe_b_router_expert, moe_w1, moe_w3, moe_w2, ple_w_gate, ple_b_gate, ple_w_proj):
    bsz, s, d = x.shape
    t = bsz * s
    depth = p.shape[0]
    h = x.reshape(t, d)
    for l in range(depth):
        if l % 2 == 0:
            e = l // 2
            h = _even_layer(h, bsz, norm_mix[l], ev_w_in[e], lru_conv_w[e], lru_conv_b[e],
                            lru_w_a[e], lru_b_a[e], lru_w_x[e], lru_b_x[e], lru_lambda[e],
                            gla_w_gate[e], gla_b_gate[e], gla_norm[e], ev_w_out[e])
        else:
            o = l // 2
            h = _odd_layer(h, bsz, norm_mix[l], od_w_in[o], s5_lambda_re[o], s5_lambda_im[o],
                           s5_log_step[o], s5_b_re[o], s5_b_im[o], s5_c_re[o], s5_c_im[o],
                           s5_d[o], od_w_glu[o])
        h = _moe_ple(h, l, norm_ffn[l], moe_w_router_group[l], moe_b_router_group[l],
                     moe_w_router_expert[l], moe_b_router_expert[l], moe_w1, moe_w3,
                     moe_w2, norm_ple[l], ple_w_gate[l], ple_b_gate[l],
                     p[l].reshape(t, p.shape[-1]), ple_w_proj[l])
    return _final_norm(h, norm_final[None]).reshape(bsz, s, d)
```

```python
import functools
import math

import jax
import jax.numpy as jnp
from jax import lax
from jax.experimental import pallas as pl
from jax.experimental.pallas import tpu as pltpu

F32 = jnp.float32
BF16 = jnp.bfloat16

D_MODEL = 2048
D_LRU = 1024
LRU_HEADS = 16
LRU_HEAD_DIM = D_LRU // LRU_HEADS
CONV_WIDTH = 4
LRU_C = 8.0
GLA_HEADS = 4
GLA_DK = 128
GLA_DV = 256
GLA_QK = GLA_HEADS * GLA_DK
GLA_V = GLA_HEADS * GLA_DV
GLA_RANK = 16
GLA_TAU = 16.0
GLA_CHUNK = 64
OFF_LRU_X = 0
OFF_LRU_G = OFF_LRU_X + D_LRU
OFF_Q = OFF_LRU_G + D_LRU
OFF_K = OFF_Q + GLA_QK
OFF_V = OFF_K + GLA_QK
OFF_OG = OFF_V + GLA_V
OFF_LR = OFF_OG + GLA_V
D_EVEN_Z = OFF_LR + GLA_QK
D_S5 = 1024
S5_GROUP = 16
S5_GROUPS = D_S5 // S5_GROUP
S5_STATE = 64
S5_MAX_RE = -1e-4
N_EXPERT_GROUPS = 4
EXPERTS_PER_GROUP = 8
N_EXPERTS = N_EXPERT_GROUPS * EXPERTS_PER_GROUP
D_EXPERT = 512
D_PLE = 256
EPS = 1e-6

LANES = 128
SUBLANES = 8
VMEM_LIMIT = 56 * 1024 * 1024

TM_MM = 1024
TN_MM = 512
TS_LRU = 256
TC_GLA = 256
TS_S5 = 256
S5_CB = 256
TM_ROUTE = 512
TM_EXP = 256
TM_PLE = 256
PLE_PIECES = 4
PLE_ROW_BUFS = 3
EXP_ROW_BUFS = 3
EXP_H_PIECES = 2
EXP_Y_PIECES = 8
SCAN_UNROLL = 8
DMA_UNROLL = 8


def _cparams(sem):
    return pltpu.CompilerParams(dimension_semantics=sem, vmem_limit_bytes=VMEM_LIMIT)


def _rms(x, g):
    return x * lax.rsqrt(jnp.mean(x * x, axis=-1, keepdims=True) + EPS) * g


def _gelu_tanh(x):
    c = math.sqrt(2.0 / math.pi)
    return 0.5 * x * (1.0 + jnp.tanh(c * (x + 0.044715 * (x * x * x))))


def _softplus(x):
    return jnp.maximum(x, 0.0) + jnp.log1p(jnp.exp(-jnp.abs(x)))


def _silu(x):
    return x * jax.nn.sigmoid(x)


def _norm_matmul_kernel(nj, x_ref, g_ref, w_ref, *rest):
    o_ref, xn_ref = rest[-2:]
    j = pl.program_id(1)

    @pl.when(j == 0)
    def _():
        xn_ref[...] = _rms(x_ref[...], g_ref[...]).astype(BF16)

    @pl.when(j < nj)
    def _():
        o_ref[...] = jnp.dot(xn_ref[...], w_ref[0].astype(BF16),
                             preferred_element_type=F32).astype(o_ref.dtype)

    if len(rest) == 3:
        @pl.when(j >= nj)
        def _():
            o_ref[...] = jnp.dot(xn_ref[...], rest[0][...].astype(BF16),
                                 preferred_element_type=F32).astype(o_ref.dtype)


def _norm_matmul(x, g, w, layer, n_main, w_extra=None):
    t, d = x.shape
    tm, tn = min(TM_MM, t), TN_MM
    nj = n_main // tn
    in_specs = [pl.BlockSpec((tm, d), lambda i, j: (i, 0)),
                pl.BlockSpec((1, d), lambda i, j: (0, 0)),
                pl.BlockSpec((1, d, tn), lambda i, j: (layer, 0, jnp.minimum(j, nj - 1)))]
    args = [x, g, w]
    n_extra = 0
    if w_extra is not None:
        n_extra = 1
        in_specs.append(pl.BlockSpec((d, tn), lambda i, j: (0, 0)))
        args.append(w_extra)
    return pl.pallas_call(
        functools.partial(_norm_matmul_kernel, nj),
        grid=(t // tm, nj + n_extra),
        in_specs=in_specs,
        out_specs=pl.BlockSpec((tm, tn), lambda i, j: (i, j)),
        out_shape=jax.ShapeDtypeStruct((t, (nj + n_extra) * tn), BF16),
        scratch_shapes=[pltpu.VMEM((tm, d), BF16)],
        compiler_params=_cparams(("parallel", "arbitrary")),
        name="norm_matmul",
    )(*args)


def _mm2_res_kernel(a1_ref, a2_ref, w_ref, x_ref, o_ref):
    k1 = a1_ref.shape[1]
    acc = jnp.dot(a1_ref[...], w_ref[0, 0:k1, :].astype(BF16), preferred_element_type=F32)
    acc = acc + jnp.dot(a2_ref[...], w_ref[0, k1:, :].astype(BF16), preferred_element_type=F32)
    o_ref[...] = x_ref[...] + acc


def _mm2_res(a1, a2, w, layer, x):
    t, k1 = a1.shape
    k2 = a2.shape[1]
    n = w.shape[2]
    tm, tn = min(TM_MM, t), TN_MM
    return pl.pallas_call(
        _mm2_res_kernel,
        grid=(t // tm, n // tn),
        in_specs=[pl.BlockSpec((tm, k1), lambda i, j: (i, 0)),
                  pl.BlockSpec((tm, k2), lambda i, j: (i, 0)),
                  pl.BlockSpec((1, k1 + k2, tn), lambda i, j: (layer, 0, j)),
                  pl.BlockSpec((tm, tn), lambda i, j: (i, j))],
        out_specs=pl.BlockSpec((tm, tn), lambda i, j: (i, j)),
        out_shape=jax.ShapeDtypeStruct((t, n), F32),
        compiler_params=_cparams(("parallel", "arbitrary")),
        name="out_proj",
    )(a1, a2, w, x)


def _glu_res_kernel(y_ref, w1_ref, w2_ref, x_ref, o_ref):
    y = y_ref[...]
    z1 = jnp.dot(y, w1_ref[0].astype(BF16), preferred_element_type=F32)
    z2 = jnp.dot(y, w2_ref[0].astype(BF16), preferred_element_type=F32)
    o_ref[...] = x_ref[...] + z1 * jax.nn.sigmoid(z2)


def _glu_res(y, w, layer, x):
    t, k = y.shape
    n = w.shape[2] // 2
    tm, tn = min(TM_MM, t), TN_MM
    nj = n // tn
    return pl.pallas_call(
        _glu_res_kernel,
        grid=(t // tm, nj),
        in_specs=[pl.BlockSpec((tm, k), lambda i, j: (i, 0)),
                  pl.BlockSpec((1, k, tn), lambda i, j: (layer, 0, j)),
                  pl.BlockSpec((1, k, tn), lambda i, j: (layer, 0, j + nj)),
                  pl.BlockSpec((tm, tn), lambda i, j: (i, j))],
        out_specs=pl.BlockSpec((tm, tn), lambda i, j: (i, j)),
        out_shape=jax.ShapeDtypeStruct((t, n), F32),
        compiler_params=_cparams(("parallel", "arbitrary")),
        name="glu_proj",
    )(y, w, w, x)


def _gate_weight_kernel(a_ref, b_ref, o_ref):
    o_ref[...] = jnp.dot(a_ref[0][:, 0:b_ref.shape[0]], b_ref[...],
                         precision=lax.Precision.HIGHEST, preferred_element_type=F32)


def _gate_weight(w_in, layer, w_gate):
    d = w_in.shape[1]
    return pl.pallas_call(
        _gate_weight_kernel,
        grid=(1,),
        in_specs=[pl.BlockSpec((1, d, LANES), lambda i: (layer, 0, OFF_LR // LANES)),
                  pl.BlockSpec(w_gate.shape, lambda i: (0, 0))],
        out_specs=pl.BlockSpec((d, w_gate.shape[1]), lambda i: (0, 0)),
        out_shape=jax.ShapeDtypeStruct((d, w_gate.shape[1]), F32),
        name="gate_weight",
    )(w_in, w_gate)


def _store_paired(dst_ref, v, bsz, ts):
    for c in range(v.shape[1] // LANES):
        for b in range(bsz):
            dst_ref.at[c // 2][pl.ds((c % 2) * bsz + b, ts, stride=SUBLANES), :] = (
                v[b * ts:(b + 1) * ts, c * LANES:(c + 1) * LANES])


def _load_paired(src_ref, bsz, ts, nchunks):
    cols = []
    for c in range(nchunks):
        cols.append(jnp.concatenate(
            [src_ref.at[c // 2][pl.ds((c % 2) * bsz + b, ts, stride=SUBLANES), :]
             for b in range(bsz)], axis=0))
    return jnp.concatenate(cols, axis=1)


def _step_rows(t0, k):
    base = pl.multiple_of(t0 * (SCAN_UNROLL * SUBLANES), SCAN_UNROLL * SUBLANES)
    return pl.ds(base + k * SUBLANES, SUBLANES)


def _pair_rows(v, batch):
    out = []
    for j in range(v.shape[1] // (2 * LANES)):
        lo = jnp.broadcast_to(v[:, (2 * j) * LANES:(2 * j + 1) * LANES], (batch, LANES))
        hi = jnp.broadcast_to(v[:, (2 * j + 1) * LANES:(2 * j + 2) * LANES], (batch, LANES))
        out.append(jnp.concatenate([lo, hi], axis=0))
    return out


def _lru_kernel(zx_ref, zg_ref, cw_ref, cb_ref, wa_ref, ba_ref, wx_ref, bx_ref, lam_ref,
                y_ref, xe_ref, a_ref, u_ref, h_ref):
    bsz, ts, c = zx_ref.shape
    rows = bsz * ts
    npair = c // (2 * LANES)
    nblk = wa_ref.shape[0]
    wblk = wa_ref.shape[1]

    @pl.when(pl.program_id(0) == 0)
    def _():
        xe_ref[:, 0:SUBLANES, :] = jnp.zeros((bsz, SUBLANES, c), F32)
        h_ref[...] = jnp.zeros(h_ref.shape, F32)

    x = zx_ref[...].astype(F32)
    xe_ref[:, SUBLANES:SUBLANES + ts, :] = x
    xc = cb_ref[...][None]
    for k in range(CONV_WIDTH):
        off = SUBLANES - (CONV_WIDTH - 1) + k
        xc = xc + cw_ref[k:k + 1, :][None] * xe_ref[:, off:off + ts, :]
    xe_ref[:, 0:SUBLANES, :] = x[:, ts - SUBLANES:, :]
    xc = xc.reshape(rows, c)

    xcb = xc.astype(BF16)
    ra, ia = [], []
    for b in range(nblk):
        xs = xcb[:, b * wblk:(b + 1) * wblk]
        ra.append(jnp.dot(xs, wa_ref[b], preferred_element_type=F32))
        ia.append(jnp.dot(xs, wx_ref[b], preferred_element_type=F32))
    r = jax.nn.sigmoid(jnp.concatenate(ra, axis=1) + ba_ref[...])
    i = jax.nn.sigmoid(jnp.concatenate(ia, axis=1) + bx_ref[...])
    log_a = (-LRU_C) * r * _softplus(-lam_ref[...])
    a = jnp.exp(log_a)
    mult = jnp.sqrt(1.0 - a * a)
    u = mult * (i * xc)
    _store_paired(a_ref, a, bsz, ts)
    _store_paired(u_ref, u, bsz, ts)

    def step(t0, hs):
        hs = list(hs)
        for k in range(SCAN_UNROLL):
            rs = _step_rows(t0, k)
            for j in range(npair):
                hs[j] = a_ref[j, rs, :] * hs[j] + u_ref[j, rs, :]
                u_ref[j, rs, :] = hs[j]
        return tuple(hs)

    hs = lax.fori_loop(0, ts // SCAN_UNROLL, step, tuple(h_ref[j] for j in range(npair)))
    for j in range(npair):
        h_ref[j] = hs[j]

    h = _load_paired(u_ref, bsz, ts, c // LANES)
    g = zg_ref[...].astype(F32).reshape(rows, c)
    y_ref[...] = (h * _gelu_tanh(g)).astype(y_ref.dtype).reshape(bsz, ts, c)


def _lru(z3, conv_w, conv_b, wa_bd, ba, wx_bd, bx, lam):
    bsz, s, _ = z3.shape
    assert 2 * bsz == SUBLANES
    ts = min(TS_LRU, s)
    c = D_LRU
    rows = bsz * ts
    full = lambda shape: pl.BlockSpec(shape, lambda i: (0,) * len(shape))
    return pl.pallas_call(
        _lru_kernel,
        grid=(s // ts,),
        in_specs=[pl.BlockSpec((bsz, ts, c), lambda i: (0, i, OFF_LRU_X // c)),
                  pl.BlockSpec((bsz, ts, c), lambda i: (0, i, OFF_LRU_G // c)),
                  full(conv_w.shape), full(conv_b.shape), full(wa_bd.shape), full(ba.shape),
                  full(wx_bd.shape), full(bx.shape), full(lam.shape)],
        out_specs=pl.BlockSpec((bsz, ts, c), lambda i: (0, i, 0)),
        out_shape=jax.ShapeDtypeStruct((bsz, s, c), BF16),
        scratch_shapes=[pltpu.VMEM((bsz, ts + SUBLANES, c), F32),
                        pltpu.VMEM((c // (2 * LANES), ts * SUBLANES, LANES), F32),
                        pltpu.VMEM((c // (2 * LANES), ts * SUBLANES, LANES), F32),
                        pltpu.VMEM((c // (2 * LANES), SUBLANES, LANES), F32)],
        compiler_params=_cparams(("arbitrary",)),
        name="rg_lru",
    )(z3, z3, conv_w, conv_b, wa_bd, ba, wx_bd, bx, lam)


def _gla_kernel(q_ref, k_ref, v_ref, og_ref, gl_ref, bg_ref, hn_ref, y_ref, st_ref):
    tc = q_ref.shape[1]
    cs = GLA_CHUNK

    @pl.when(pl.program_id(1) == 0)
    def _():
        st_ref[...] = jnp.zeros(st_ref.shape, F32)

    row = lax.broadcasted_iota(jnp.int32, (tc, tc), 0)
    col = lax.broadcasted_iota(jnp.int32, (tc, tc), 1)
    blk = jnp.where(row // cs == col // cs, 1.0, 0.0)
    tri = jnp.where(row >= col, blk, 0.0)
    causal = tri > 0.0
    for h in range(GLA_HEADS):
        dk = slice(h * GLA_DK, (h + 1) * GLA_DK)
        dv = slice(h * GLA_DV, (h + 1) * GLA_DV)
        x = gl_ref[0, :, dk].astype(F32) + bg_ref[:, dk]
        gk = (jnp.minimum(x, 0.0) - jnp.log1p(jnp.exp(-jnp.abs(x)))) * (1.0 / GLA_TAU)
        bcum = jnp.dot(tri, gk, precision=lax.Precision.HIGHEST, preferred_element_type=F32)
        btot = jnp.dot(blk, gk, precision=lax.Precision.HIGHEST, preferred_element_type=F32)
        q = q_ref[0, :, dk].astype(F32) * (GLA_DK ** -0.5)
        k = k_ref[0, :, dk].astype(F32)
        v = v_ref[0, :, dv]
        qs = (q * jnp.exp(bcum)).astype(BF16)
        ks = (k * jnp.exp(-bcum)).astype(BF16)
        ke = (k * jnp.exp(btot - bcum)).astype(BF16)
        att = lax.dot_general(qs, ks, (((1,), (1,)), ((), ())), preferred_element_type=F32)
        att = jnp.where(causal, att, 0.0).astype(BF16)
        o_intra = jnp.dot(att, v, preferred_element_type=F32)
        st = st_ref[h]
        outs = []
        for c in range(tc // cs):
            sl = slice(c * cs, (c + 1) * cs)
            outs.append(o_intra[sl] + lax.dot_general(
                qs[sl], st.astype(BF16), (((1,), (1,)), ((), ())), preferred_element_type=F32))
            ds = lax.dot_general(v[sl], ke[sl], (((0,), (0,)), ((), ())),
                                 preferred_element_type=F32)
            st = jnp.exp(btot[c * cs:c * cs + 1, :]) * st + ds
        st_ref[h] = st
        o = jnp.concatenate(outs, axis=0)
        o = o * lax.rsqrt(jnp.mean(o * o, axis=-1, keepdims=True) + EPS)
        y = (o * hn_ref[:, dv]) * _silu(og_ref[0, :, dv].astype(F32))
        y_ref[0, :, dv] = y.astype(y_ref.dtype)


def _gla(z3, b_gate, head_norm):
    bsz, s, _ = z3.shape
    tc = min(TC_GLA, s)
    qb, kb, vb, ob, gb = (OFF_Q // GLA_QK, OFF_K // GLA_QK, OFF_V // GLA_V, OFF_OG // GLA_V,
                          OFF_LR // GLA_QK)
    return pl.pallas_call(
        _gla_kernel,
        grid=(bsz, s // tc),
        in_specs=[pl.BlockSpec((1, tc, GLA_QK), lambda b, i: (b, i, qb)),
                  pl.BlockSpec((1, tc, GLA_QK), lambda b, i: (b, i, kb)),
                  pl.BlockSpec((1, tc, GLA_V), lambda b, i: (b, i, vb)),
                  pl.BlockSpec((1, tc, GLA_V), lambda b, i: (b, i, ob)),
                  pl.BlockSpec((1, tc, GLA_QK), lambda b, i: (b, i, gb)),
                  pl.BlockSpec((1, GLA_QK), lambda b, i: (0, 0)),
                  pl.BlockSpec((1, GLA_V), lambda b, i: (0, 0))],
        out_specs=pl.BlockSpec((1, tc, GLA_V), lambda b, i: (b, i, 0)),
        out_shape=jax.ShapeDtypeStruct((bsz, s, GLA_V), BF16),
        scratch_shapes=[pltpu.VMEM((GLA_HEADS, GLA_DV, GLA_DK), F32)],
        compiler_params=_cparams(("parallel", "arbitrary")),
        name="gla",
    )(z3, z3, z3, z3, z3, b_gate, head_norm)


def _s5_kernel(u_ref, lre_ref, lim_ref, lst_ref, bre_ref, bim_ref, cre_ref, cim_ref, d_ref,
               y_ref, wbr_ref, wbi_ref, ar_ref, ai_ref, sr_ref, si_ref, hr_ref, hi_ref):
    bsz, ts, cb = u_ref.shape
    rows = bsz * ts
    nst = lre_ref.shape[2]
    npair = nst // (2 * LANES)

    @pl.when(pl.program_id(1) == 0)
    def _():
        lr = jnp.minimum(lre_ref[0], S5_MAX_RE)
        li = lim_ref[0]
        dt = jnp.exp(lst_ref[0])
        mag = jnp.exp(lr * dt)
        ab_re = mag * jnp.cos(li * dt)
        ab_im = mag * jnp.sin(li * dt)
        den = lr * lr + li * li
        coef_re = ((ab_re - 1.0) * lr + ab_im * li) / den
        coef_im = (ab_im * lr - (ab_re - 1.0) * li) / den
        bre = bre_ref[0]
        bim = bim_ref[0]
        wbr_ref[...] = (coef_re * bre - coef_im * bim).astype(BF16)
        wbi_ref[...] = (coef_re * bim + coef_im * bre).astype(BF16)
        for j, v in enumerate(_pair_rows(ab_re, bsz)):
            ar_ref[j] = v
        for j, v in enumerate(_pair_rows(ab_im, bsz)):
            ai_ref[j] = v
        hr_ref[...] = jnp.zeros(hr_ref.shape, F32)
        hi_ref[...] = jnp.zeros(hi_ref.shape, F32)

    u = u_ref[...].reshape(rows, cb)
    _store_paired(sr_ref, jnp.dot(u, wbr_ref[...], preferred_element_type=F32), bsz, ts)
    _store_paired(si_ref, jnp.dot(u, wbi_ref[...], preferred_element_type=F32), bsz, ts)

    ars = [ar_ref[j] for j in range(npair)]
    ais = [ai_ref[j] for j in range(npair)]

    def step(t0, carry):
        hr, hi = list(carry[0]), list(carry[1])
        for k in range(SCAN_UNROLL):
            rs = _step_rows(t0, k)
            for j in range(npair):
                nr = ars[j] * hr[j] - ais[j] * hi[j] + sr_ref[j, rs, :]
                ni = ars[j] * hi[j] + ais[j] * hr[j] + si_ref[j, rs, :]
                hr[j], hi[j] = nr, ni
                sr_ref[j, rs, :] = nr
                si_ref[j, rs, :] = ni
        return tuple(hr), tuple(hi)

    init = (tuple(hr_ref[j] for j in range(npair)), tuple(hi_ref[j] for j in range(npair)))
    hr, hi = lax.fori_loop(0, ts // SCAN_UNROLL, step, init)
    for j in range(npair):
        hr_ref[j] = hr[j]
        hi_ref[j] = hi[j]

    h_re = _load_paired(sr_ref, bsz, ts, nst // LANES).astype(BF16)
    h_im = _load_paired(si_ref, bsz, ts, nst // LANES).astype(BF16)
    y = jnp.dot(h_re, cre_ref[0], preferred_element_type=F32)
    y = y - jnp.dot(h_im, cim_ref[0], preferred_element_type=F32)
    y = y + d_ref[...] * u.astype(F32)
    y_ref[...] = _gelu_tanh(y).astype(y_ref.dtype).reshape(bsz, ts, cb)


def _s5(u3, lre, lim, lst, bre_bd, bim_bd, cre_bd, cim_bd, d_skip):
    bsz, s, c = u3.shape
    assert 2 * bsz == SUBLANES
    ts = min(TS_S5, s)
    cb = S5_CB
    ncb = c // cb
    nst = lre.shape[2]
    rows = bsz * ts
    npair = nst // (2 * LANES)
    blk = lambda shape: pl.BlockSpec((1,) + shape, lambda ci, i: (ci, 0, 0))
    return pl.pallas_call(
        _s5_kernel,
        grid=(ncb, s // ts),
        in_specs=[pl.BlockSpec((bsz, ts, cb), lambda ci, i: (0, i, ci)),
                  blk((1, nst)), blk((1, nst)), blk((1, nst)),
                  blk((cb, nst)), blk((cb, nst)), blk((nst, cb)), blk((nst, cb)),
                  pl.BlockSpec((1, cb), lambda ci, i: (0, ci))],
        out_specs=pl.BlockSpec((bsz, ts, cb), lambda ci, i: (0, i, ci)),
        out_shape=jax.ShapeDtypeStruct((bsz, s, c), BF16),
        scratch_shapes=[pltpu.VMEM((cb, nst), BF16), pltpu.VMEM((cb, nst), BF16),
                        pltpu.VMEM((npair, SUBLANES, LANES), F32),
                        pltpu.VMEM((npair, SUBLANES, LANES), F32),
                        pltpu.VMEM((npair, ts * SUBLANES, LANES), F32),
                        pltpu.VMEM((npair, ts * SUBLANES, LANES), F32),
                        pltpu.VMEM((npair, SUBLANES, LANES), F32),
                        pltpu.VMEM((npair, SUBLANES, LANES), F32)],
        compiler_params=_cparams(("parallel", "arbitrary")),
        name="s5",
    )(u3, lre, lim, lst, bre_bd, bim_bd, cre_bd, cim_bd, d_skip)


ROUTE_GROUP_LANE0 = 0
ROUTE_EXPERT_LANE0 = N_EXPERT_GROUPS
RI_E0, RI_E1, RI_C0, RI_C1, RI_R0, RI_R1 = 0, 1, 2, 3, 4, 5


def _router_kernel(x_ref, g_ref, w_ref, b_ref, xn_ref, ri_ref, cnt_ref, run_ref):
    tm = x_ref.shape[0]
    neg = -jnp.inf

    @pl.when(pl.program_id(0) == 0)
    def _():
        run_ref[...] = jnp.zeros(run_ref.shape, F32)

    xn = _rms(x_ref[...], g_ref[...])
    xn_ref[...] = xn
    logits = jnp.dot(xn, w_ref[...], precision=lax.Precision.HIGHEST,
                     preferred_element_type=F32) + b_ref[...]
    lane = lax.broadcasted_iota(jnp.int32, (tm, LANES), 1)

    def first_lane(mask):
        return jnp.min(jnp.where(mask, lane, LANES), axis=1, keepdims=True)

    lg = jnp.where(lane < N_EXPERT_GROUPS, logits, neg)
    mg = jnp.max(lg, axis=1, keepdims=True)
    sg = jnp.sum(jnp.exp(lg - mg), axis=1, keepdims=True)
    gate_g = 1.0 / sg
    g_idx = first_lane(lg == mg)
    lo = ROUTE_EXPERT_LANE0 + EXPERTS_PER_GROUP * g_idx
    in_group = jnp.abs(2 * (lane - lo) - (EXPERTS_PER_GROUP - 1)) < EXPERTS_PER_GROUP
    le = jnp.where(in_group, logits, neg)
    m1 = jnp.max(le, axis=1, keepdims=True)
    i1 = first_lane(le == m1)
    le2 = jnp.where(lane == i1, neg, le)
    m2 = jnp.max(le2, axis=1, keepdims=True)
    i2 = first_lane(le2 == m2)
    se = jnp.sum(jnp.exp(le - m1), axis=1, keepdims=True)
    p1 = 1.0 / se
    p2 = jnp.exp(m2 - m1) / se
    c0 = gate_g * (p1 / (p1 + p2))
    c1 = gate_g * (p2 / (p1 + p2))

    sel0 = lane == i1
    sel1 = lane == i2
    onehot = jnp.where(sel0, 1.0, jnp.where(sel1, 1.0, 0.0))
    r_i = lax.broadcasted_iota(jnp.int32, (tm, tm), 0)
    c_i = lax.broadcasted_iota(jnp.int32, (tm, tm), 1)
    before = jnp.where(c_i < r_i, 1.0, 0.0).astype(BF16)
    prefix = jnp.dot(before, onehot.astype(BF16), preferred_element_type=F32) + run_ref[...]
    rank0 = jnp.sum(jnp.where(sel0, prefix, 0.0), axis=1, keepdims=True)
    rank1 = jnp.sum(jnp.where(sel1, prefix, 0.0), axis=1, keepdims=True)
    run_ref[...] = run_ref[...] + jnp.sum(onehot, axis=0, keepdims=True)
    cnt_ref[...] = run_ref[...]

    e0 = (i1 - ROUTE_EXPERT_LANE0).astype(F32)
    e1 = (i2 - ROUTE_EXPERT_LANE0).astype(F32)
    rec = jnp.zeros((tm, LANES), F32)
    for ln, val in ((RI_E0, e0), (RI_E1, e1), (RI_C0, c0), (RI_C1, c1), (RI_R0, rank0),
                    (RI_R1, rank1)):
        rec = jnp.where(lane == ln, val, rec)
    ri_ref[...] = rec


def _router(x, g, w_route, b_route):
    t, d = x.shape
    tm = min(TM_ROUTE, t)
    return pl.pallas_call(
        _router_kernel,
        grid=(t // tm,),
        in_specs=[pl.BlockSpec((tm, d), lambda i: (i, 0)),
                  pl.BlockSpec((1, d), lambda i: (0, 0)),
                  pl.BlockSpec((d, LANES), lambda i: (0, 0)),
                  pl.BlockSpec((1, LANES), lambda i: (0, 0))],
        out_specs=[pl.BlockSpec((tm, d), lambda i: (i, 0)),
                   pl.BlockSpec((tm, LANES), lambda i: (i, 0)),
                   pl.BlockSpec((1, LANES), lambda i: (0, 0))],
        out_shape=[jax.ShapeDtypeStruct((t, d), F32),
                   jax.ShapeDtypeStruct((t, LANES), F32),
                   jax.ShapeDtypeStruct((1, LANES), F32)],
        scratch_shapes=[pltpu.VMEM((1, LANES), F32)],
        compiler_params=_cparams(("arbitrary",)),
        name="router",
    )(x, g, w_route, b_route)


def _piece_groups(n_rows, weights):
    total = sum(weights)
    bounds = [round(n_rows * sum(weights[:k]) / total) for k in range(len(weights) + 1)]
    return [range(bounds[k], bounds[k + 1]) for k in range(len(weights))]


def _experts_kernel(layer, eidx_ref, eseq_ref, meta_ref, tok_ref, xn_hbm, w1_hbm, w3_hbm,
                    w2_hbm, y_ref, ws1, ws3, ws2, w1b_ref, w3b_ref, w2b_ref, xbuf, hbuf,
                    gsem, wsem):
    i = pl.program_id(0)
    nv = meta_ref[0]
    ne = meta_ref[1]
    tme, d = y_ref.shape
    f = hbuf.shape[1]
    slot = i % EXP_ROW_BUFS

    def gather_row(tile, r, s):
        pltpu.make_async_copy(xn_hbm.at[pl.ds(tok_ref[tile * tme + r], 1), :],
                              xbuf.at[s, pl.ds(r, 1), :], gsem.at[s]).start()

    def gather_tile(tile, s):
        def body(r0, c):
            for k in range(DMA_UNROLL):
                gather_row(tile, r0 * DMA_UNROLL + k, s)
            return c

        lax.fori_loop(0, tme // DMA_UNROLL, body, 0)

    def gather_wait(s):
        pltpu.make_async_copy(xn_hbm.at[pl.ds(0, tme), :], xbuf.at[s], gsem.at[s]).wait()

    def weight_copies(k):
        e = eseq_ref[k]
        s = k % 2
        return [pltpu.make_async_copy(w_hbm.at[layer, e], ws.at[s], wsem.at[s])
                for w_hbm, ws in ((w1_hbm, ws1), (w3_hbm, ws3), (w2_hbm, ws2))]

    @pl.when(i == 0)
    def _():
        for c in weight_copies(0):
            c.start()

        @pl.when(ne > 1)
        def _():
            for c in weight_copies(1):
                c.start()

        gather_tile(0, 0)
        gather_tile(jnp.minimum(1, nv - 1), 1)

    @pl.when(i >= nv)
    def _():
        y_ref[...] = jnp.zeros(y_ref.shape, F32)

    @pl.when(i < nv)
    def _():
        gather_wait(slot)
        k = eidx_ref[i]

        @pl.when(jnp.logical_or(i == 0, k != eidx_ref[jnp.maximum(i - 1, 0)]))
        def _():
            for c in weight_copies(k):
                c.wait()
            s = k % 2
            w1b_ref[...] = ws1[s].astype(BF16)
            w3b_ref[...] = ws3[s].astype(BF16)
            w2b_ref[...] = ws2[s].astype(BF16)

            @pl.when(k + 2 < ne)
            def _():
                for c in weight_copies(k + 2):
                    c.start()

        nxt = jnp.minimum(i + 2, nv - 1)
        other = (i + 2) % EXP_ROW_BUFS
        n_h, n_y = EXP_H_PIECES, EXP_Y_PIECES
        groups = _piece_groups(tme, [2 * n_y] * n_h + [n_h] * n_y)

        xb = xbuf[slot].astype(BF16)
        hw = f // n_h
        for c in range(n_h):
            cs = slice(c * hw, (c + 1) * hw)
            h1 = jnp.dot(xb, w1b_ref[:, cs], preferred_element_type=F32)
            h3 = jnp.dot(xb, w3b_ref[:, cs], preferred_element_type=F32)
            hbuf[:, cs] = (_silu(h1) * h3).astype(BF16)
            for r in groups[c]:
                gather_row(nxt, r, other)
        hb = hbuf[...]
        yw = d // n_y
        for c in range(n_y):
            cs = slice(c * yw, (c + 1) * yw)
            y_ref[:, cs] = jnp.dot(hb, w2b_ref[:, cs], preferred_element_type=F32)
            for r in groups[n_h + c]:
                gather_row(nxt, r, other)

        @pl.when(i == nv - 1)
        def _():
            gather_wait((i + 1) % EXP_ROW_BUFS)
            gather_wait(other)


def _experts(tile_eidx, expert_seq, meta, slot_tok, xn, w1, w3, w2, layer):
    t, d = xn.shape
    f = w1.shape[-1]
    nt = tile_eidx.shape[0]
    tme = slot_tok.shape[0] // nt
    hbm = pl.BlockSpec(memory_space=pl.ANY)
    return pl.pallas_call(
        functools.partial(_experts_kernel, layer),
        grid_spec=pltpu.PrefetchScalarGridSpec(
            num_scalar_prefetch=4,
            grid=(nt,),
            in_specs=[hbm, hbm, hbm, hbm],
            out_specs=pl.BlockSpec((tme, d), lambda i, *_: (i, 0)),
            scratch_shapes=[pltpu.VMEM((2, d, f), F32), pltpu.VMEM((2, d, f), F32),
                            pltpu.VMEM((2, f, d), F32),
                            pltpu.VMEM((d, f), BF16), pltpu.VMEM((d, f), BF16),
                            pltpu.VMEM((f, d), BF16),
                            pltpu.VMEM((EXP_ROW_BUFS, tme, d), F32),
                            pltpu.VMEM((tme, f), BF16),
                            pltpu.SemaphoreType.DMA((EXP_ROW_BUFS,)),
                            pltpu.SemaphoreType.DMA((2,))]),
        out_shape=jax.ShapeDtypeStruct((nt * tme, d), F32),
        compiler_params=_cparams(("arbitrary",)),
        name="experts",
    )(tile_eidx, expert_seq, meta, slot_tok, xn, w1, w3, w2)


def _route_plan(rinfo, counts, tme):
    t = rinfo.shape[0]
    nt = 2 * t // tme + N_EXPERTS
    e = rinfo[:, RI_E0:RI_E1 + 1].astype(jnp.int32)
    rank = rinfo[:, RI_R0:RI_R1 + 1].astype(jnp.int32)
    cnt = counts[0, ROUTE_EXPERT_LANE0:ROUTE_EXPERT_LANE0 + N_EXPERTS].astype(jnp.int32)
    padded = ((cnt + tme - 1) // tme) * tme
    ends = jnp.cumsum(padded)
    offs = ends - padded
    pos = offs[e] + rank
    n_valid = (ends[-1] // tme).astype(jnp.int32)
    starts = jnp.arange(nt, dtype=jnp.int32) * tme
    tile_expert = jnp.sum((starts[:, None] >= ends[None, :]).astype(jnp.int32), axis=1)
    last = jnp.take(tile_expert, jnp.maximum(n_valid - 1, 0))
    tile_expert = jnp.where(jnp.arange(nt) < n_valid, tile_expert, last)
    tile_expert = jnp.minimum(tile_expert, N_EXPERTS - 1).astype(jnp.int32)
    used_cum = jnp.cumsum((cnt > 0).astype(jnp.int32))
    n_used = used_cum[-1]
    ordinals = jnp.arange(N_EXPERTS, dtype=jnp.int32)
    expert_seq = jnp.sum((used_cum[None, :] <= ordinals[:, None]).astype(jnp.int32), axis=1)
    expert_seq = jnp.minimum(expert_seq, N_EXPERTS - 1).astype(jnp.int32)
    tile_eidx = (jnp.take(used_cum, tile_expert) - 1).astype(jnp.int32)
    meta = jnp.stack([n_valid, n_used]).astype(jnp.int32)
    tok = jnp.arange(t, dtype=jnp.int32)
    pos_flat = pos.T.reshape(-1)
    slot_tok = jnp.zeros((nt * tme,), jnp.int32).at[pos_flat].set(jnp.tile(tok, 2))
    return tile_eidx, expert_seq, meta, slot_tok, pos_flat


def _ple_kernel(final, pos_ref, ys_hbm, x_ref, ri_ref, g_ref, wg_ref, bg_ref, p_ref, wp_ref,
                gf_ref, o_ref, ybuf, sem):
    i = pl.program_id(0)
    nb = pl.num_programs(0)
    tm, d = o_ref.shape
    t = nb * tm
    slot = i % PLE_ROW_BUFS

    def gather_row(blk, r, s):
        for k in range(2):
            src = pos_ref[k * t + blk * tm + r]
            pltpu.make_async_copy(ys_hbm.at[pl.ds(src, 1), :], ybuf.at[s, k, pl.ds(r, 1), :],
                                  sem.at[s]).start()

    def gather_block(blk, s):
        def body(r0, c):
            for k in range(DMA_UNROLL):
                gather_row(blk, r0 * DMA_UNROLL + k, s)
            return c

        lax.fori_loop(0, tm // DMA_UNROLL, body, 0)

    def gather_wait(s):
        for k in range(2):
            pltpu.make_async_copy(ys_hbm.at[pl.ds(0, tm), :], ybuf.at[s, k], sem.at[s]).wait()

    @pl.when(i == 0)
    def _():
        gather_block(0, 0)
        gather_block(jnp.minimum(1, nb - 1), 1)

    gather_wait(slot)
    nxt = jnp.minimum(i + 2, nb - 1)
    other = (i + 2) % PLE_ROW_BUFS
    groups = _piece_groups(tm, [1] * PLE_PIECES)
    ri = ri_ref[...]
    x2 = x_ref[...] + ri[:, RI_C0:RI_C0 + 1] * ybuf[slot, 0] + ri[:, RI_C1:RI_C1 + 1] * ybuf[slot, 1]
    o_ref[...] = x2
    xn = _rms(x2, g_ref[...]).astype(BF16)
    pb = p_ref[0].astype(BF16)
    w = d // PLE_PIECES
    for c in range(PLE_PIECES):
        cs = slice(c * w, (c + 1) * w)
        gate = jax.nn.sigmoid(jnp.dot(xn, wg_ref[0, :, cs], preferred_element_type=F32)
                              + bg_ref[:, cs])
        proj = jnp.dot(pb, wp_ref[0, :, cs], preferred_element_type=F32)
        o_ref[:, cs] = o_ref[:, cs] + gate * proj
        for r in groups[c]:
            gather_row(nxt, r, other)
    if final:
        o_ref[...] = _rms(o_ref[...], gf_ref[...])

    @pl.when(i == nb - 1)
    def _():
        gather_wait((i + 1) % PLE_ROW_BUFS)
        gather_wait(other)


def _ple(x, ys, pos_flat, rinfo, g, wg, bg, p, wp, layer, g_final):
    t, d = x.shape
    tm = min(TM_PLE, t)
    dp = p.shape[-1]
    final = layer == p.shape[0] - 1
    return pl.pallas_call(
        functools.partial(_ple_kernel, final),
        grid_spec=pltpu.PrefetchScalarGridSpec(
            num_scalar_prefetch=1,
            grid=(t // tm,),
            in_specs=[pl.BlockSpec(memory_space=pl.ANY),
                      pl.BlockSpec((tm, d), lambda i, pos: (i, 0)),
                      pl.BlockSpec((tm, LANES), lambda i, pos: (i, 0)),
                      pl.BlockSpec((1, d), lambda i, pos: (0, 0)),
                      pl.BlockSpec((1, d, d), lambda i, pos: (layer, 0, 0)),
                      pl.BlockSpec((1, d), lambda i, pos: (0, 0)),
                      pl.BlockSpec((1, tm, dp), lambda i, pos: (layer, i, 0)),
                      pl.BlockSpec((1, dp, d), lambda i, pos: (layer, 0, 0)),
                      pl.BlockSpec((1, d), lambda i, pos: (0, 0))],
            out_specs=pl.BlockSpec((tm, d), lambda i, pos: (i, 0)),
            scratch_shapes=[pltpu.VMEM((PLE_ROW_BUFS, 2, tm, d), F32),
                            pltpu.SemaphoreType.DMA((PLE_ROW_BUFS,))]),
        out_shape=jax.ShapeDtypeStruct((t, d), F32),
        compiler_params=_cparams(("arbitrary",)),
        name="ple",
    )(pos_flat, ys, x, rinfo, g, wg, bg, p, wp, g_final)


def _moe_ple(x, layer, g_ffn, w_rg, b_rg, w_re, b_re, w1, w3, w2, g_ple, wg, bg, p, wp,
             g_final):
    d = x.shape[1]
    pad = LANES - N_EXPERT_GROUPS - N_EXPERTS
    w_route = jnp.concatenate([w_rg, w_re, jnp.zeros((d, pad), F32)], axis=1)
    b_route = jnp.concatenate([b_rg, b_re, jnp.zeros((pad,), F32)])[None]
    xn, rinfo, counts = _router(x, g_ffn[None], w_route, b_route)
    tile_eidx, expert_seq, meta, slot_tok, pos_flat = _route_plan(rinfo, counts, TM_EXP)
    ys = _experts(tile_eidx, expert_seq, meta, slot_tok, xn, w1, w3, w2, layer)
    return _ple(x, ys, pos_flat, rinfo, g_ple[None], wg, bg[None], p, wp, layer, g_final[None])


def _block_diag(w, nb):
    h, d, _ = w.shape
    w4 = w.reshape(h // nb, nb, d, d)
    out = jnp.einsum('cadk,ab->cadbk', w4, jnp.eye(nb, dtype=w.dtype))
    return out.reshape(h // nb, nb * d, nb * d)


def _even_layer(x, bsz, e, g, w_in, conv_w, conv_b, w_a, b_a, w_x, b_x, lam, w_gate, b_gate,
                head_norm, w_out):
    t = x.shape[0]
    w_gk = _gate_weight(w_in, e, w_gate)
    z = _norm_matmul(x, g[None], w_in, e, OFF_LR, w_gk)
    z3 = z.reshape(bsz, t // bsz, D_EVEN_Z)
    heads_per_blk = 2 * LANES // LRU_HEAD_DIM
    y_lru = _lru(z3, conv_w, conv_b[None], _block_diag(w_a, heads_per_blk).astype(BF16),
                 b_a[None], _block_diag(w_x, heads_per_blk).astype(BF16), b_x[None], lam[None])
    y_gla = _gla(z3, b_gate[None], head_norm[None])
    return _mm2_res(y_lru.reshape(t, D_LRU), y_gla.reshape(t, GLA_V), w_out, e, x)


def _odd_layer(x, bsz, o, g, w_in, lam_re, lam_im, log_step, b_re, b_im, c_re, c_im, d_skip,
               w_glu):
    t = x.shape[0]
    u = _norm_matmul(x, g[None], w_in, o, D_S5)
    gpb = S5_CB // S5_GROUP
    ncb = S5_GROUPS // gpb
    nst = gpb * S5_STATE
    eye = jnp.eye(gpb, dtype=F32)
    lay = lambda a: a.reshape(ncb, 1, nst)
    lst = jnp.repeat(log_step, S5_STATE)
    bexp = lambda b: jnp.einsum('cgph,gk->cghkp', b.reshape(ncb, gpb, S5_STATE, S5_GROUP),
                                eye).reshape(ncb, S5_CB, nst)
    cexp = lambda c: jnp.einsum('cgop,gk->cgpko', c.reshape(ncb, gpb, S5_GROUP, S5_STATE),
                                eye).reshape(ncb, nst, S5_CB).astype(BF16)
    y = _s5(u.reshape(bsz, t // bsz, D_S5), lay(lam_re), lay(lam_im), lay(lst),
            bexp(b_re), bexp(b_im), cexp(c_re), cexp(c_im), d_skip[None])
    return _glu_res(y.reshape(t, D_S5), w_glu, o, x)


def kernel(x, p, norm_mix, norm_ffn, norm_ple, norm_final, ev_w_in, lru_conv_w, lru_conv_b, lru_w_a, lru_b_a, lru_w_x, lru_b_x, lru_lambda, gla_w_gate, gla_b_gate, gla_norm, ev_w_out, od_w_in, s5_lambda_re, s5_lambda_im, s5_log_step, s5_b_re, s5_b_im, s5_c_re, s5_c_im, s5_d, od_w_glu, moe_w_router_group, moe_b_router_group, moe_w_router_expert, moe_b_router_expert, moe_w1, moe_w3, moe_w2, ple_w_gate, ple_b_gate, ple_w_proj):
    bsz, s, d = x.shape
    t = bsz * s
    depth = p.shape[0]
    h = x.reshape(t, d)
    p3 = p.reshape(depth, t, p.shape[-1])
    wg_bf = ple_w_gate.astype(BF16)
    wp_bf = ple_w_proj.astype(BF16)
    for l in range(depth):
        if l % 2 == 0:
            e = l // 2
            h = _even_layer(h, bsz, e, norm_mix[l], ev_w_in, lru_conv_w[e], lru_conv_b[e],
                            lru_w_a[e], lru_b_a[e], lru_w_x[e], lru_b_x[e], lru_lambda[e],
                            gla_w_gate[e], gla_b_gate[e], gla_norm[e], ev_w_out)
        else:
            o = l // 2
            h = _odd_layer(h, bsz, o, norm_mix[l], od_w_in, s5_lambda_re[o], s5_lambda_im[o],
                           s5_log_step[o], s5_b_re[o], s5_b_im[o], s5_c_re[o], s5_c_im[o],
                           s5_d[o], od_w_glu)
        h = _moe_ple(h, l, norm_ffn[l], moe_w_router_group[l], moe_b_router_group[l],
                     moe_w_router_expert[l], moe_b_router_expert[l], moe_w1, moe_w3,
                     moe_w2, norm_ple[l], wg_bf, ple_b_gate[l], p3, wp_bf, norm_final)
    return h.reshape(bsz, s, d)
```

```python
import functools
import math

import jax
import jax.numpy as jnp
from jax import lax
from jax.experimental import pallas as pl
from jax.experimental.pallas import tpu as pltpu

F32 = jnp.float32
BF16 = jnp.bfloat16

D_MODEL = 2048
D_LRU = 1024
LRU_HEADS = 16
LRU_HEAD_DIM = D_LRU // LRU_HEADS
CONV_WIDTH = 4
LRU_C = 8.0
GLA_HEADS = 4
GLA_DK = 128
GLA_DV = 256
GLA_QK = GLA_HEADS * GLA_DK
GLA_V = GLA_HEADS * GLA_DV
GLA_RANK = 16
GLA_TAU = 16.0
GLA_CHUNK = 64
OFF_LRU_X = 0
OFF_LRU_G = OFF_LRU_X + D_LRU
OFF_Q = OFF_LRU_G + D_LRU
OFF_K = OFF_Q + GLA_QK
OFF_V = OFF_K + GLA_QK
OFF_OG = OFF_V + GLA_V
OFF_LR = OFF_OG + GLA_V
D_EVEN_Z = OFF_LR + GLA_QK
D_S5 = 1024
S5_GROUP = 16
S5_GROUPS = D_S5 // S5_GROUP
S5_STATE = 64
S5_MAX_RE = -1e-4
N_EXPERT_GROUPS = 4
EXPERTS_PER_GROUP = 8
N_EXPERTS = N_EXPERT_GROUPS * EXPERTS_PER_GROUP
D_EXPERT = 512
D_PLE = 256
EPS = 1e-6

LANES = 128
SUBLANES = 8
VMEM_LIMIT = 56 * 1024 * 1024

TM_MM = 1024
TN_MM = 512
TS_LRU = 256
TC_GLA = 256
TS_S5 = 256
S5_CB = 256
TM_ROUTE = 512
TM_EXP = 256
TM_PLE = 256
PLE_PIECES = 4
PLE_ROW_BUFS = 3
TM_DISPATCH = 256
DISPATCH_BUFS = 3
WEIGHT_DMA_PRIORITY = 1
SCAN_UNROLL = 8
DMA_UNROLL = 8


def _cparams(sem):
    return pltpu.CompilerParams(dimension_semantics=sem, vmem_limit_bytes=VMEM_LIMIT)


def _rms(x, g):
    return x * lax.rsqrt(jnp.mean(x * x, axis=-1, keepdims=True) + EPS) * g


def _gelu_tanh(x):
    c = math.sqrt(2.0 / math.pi)
    return 0.5 * x * (1.0 + jnp.tanh(c * (x + 0.044715 * (x * x * x))))


def _softplus(x):
    return jnp.maximum(x, 0.0) + jnp.log1p(jnp.exp(-jnp.abs(x)))


def _silu(x):
    return x * jax.nn.sigmoid(x)


def _norm_matmul_kernel(nj, x_ref, g_ref, w_ref, *rest):
    o_ref, xn_ref = rest[-2:]
    j = pl.program_id(1)

    @pl.when(j == 0)
    def _():
        xn_ref[...] = _rms(x_ref[...], g_ref[...]).astype(BF16)

    @pl.when(j < nj)
    def _():
        o_ref[...] = jnp.dot(xn_ref[...], w_ref[0].astype(BF16),
                             preferred_element_type=F32).astype(o_ref.dtype)

    if len(rest) == 3:
        @pl.when(j >= nj)
        def _():
            o_ref[...] = jnp.dot(xn_ref[...], rest[0][...].astype(BF16),
                                 preferred_element_type=F32).astype(o_ref.dtype)


def _norm_matmul(x, g, w, layer, n_main, w_extra=None):
    t, d = x.shape
    tm, tn = min(TM_MM, t), TN_MM
    nj = n_main // tn
    in_specs = [pl.BlockSpec((tm, d), lambda i, j: (i, 0)),
                pl.BlockSpec((1, d), lambda i, j: (0, 0)),
                pl.BlockSpec((1, d, tn), lambda i, j: (layer, 0, jnp.minimum(j, nj - 1)))]
    args = [x, g, w]
    n_extra = 0
    if w_extra is not None:
        n_extra = 1
        in_specs.append(pl.BlockSpec((d, tn), lambda i, j: (0, 0)))
        args.append(w_extra)
    return pl.pallas_call(
        functools.partial(_norm_matmul_kernel, nj),
        grid=(t // tm, nj + n_extra),
        in_specs=in_specs,
        out_specs=pl.BlockSpec((tm, tn), lambda i, j: (i, j)),
        out_shape=jax.ShapeDtypeStruct((t, (nj + n_extra) * tn), BF16),
        scratch_shapes=[pltpu.VMEM((tm, d), BF16)],
        compiler_params=_cparams(("parallel", "arbitrary")),
        name="norm_matmul",
    )(*args)


def _mm2_res_kernel(a1_ref, a2_ref, w_ref, x_ref, o_ref):
    k1 = a1_ref.shape[1]
    acc = jnp.dot(a1_ref[...], w_ref[0, 0:k1, :].astype(BF16), preferred_element_type=F32)
    acc = acc + jnp.dot(a2_ref[...], w_ref[0, k1:, :].astype(BF16), preferred_element_type=F32)
    o_ref[...] = x_ref[...] + acc


def _mm2_res(a1, a2, w, layer, x):
    t, k1 = a1.shape
    k2 = a2.shape[1]
    n = w.shape[2]
    tm, tn = min(TM_MM, t), TN_MM
    return pl.pallas_call(
        _mm2_res_kernel,
        grid=(t // tm, n // tn),
        in_specs=[pl.BlockSpec((tm, k1), lambda i, j: (i, 0)),
                  pl.BlockSpec((tm, k2), lambda i, j: (i, 0)),
                  pl.BlockSpec((1, k1 + k2, tn), lambda i, j: (layer, 0, j)),
                  pl.BlockSpec((tm, tn), lambda i, j: (i, j))],
        out_specs=pl.BlockSpec((tm, tn), lambda i, j: (i, j)),
        out_shape=jax.ShapeDtypeStruct((t, n), F32),
        compiler_params=_cparams(("parallel", "arbitrary")),
        name="out_proj",
    )(a1, a2, w, x)


def _glu_res_kernel(y_ref, w1_ref, w2_ref, x_ref, o_ref):
    y = y_ref[...]
    z1 = jnp.dot(y, w1_ref[0].astype(BF16), preferred_element_type=F32)
    z2 = jnp.dot(y, w2_ref[0].astype(BF16), preferred_element_type=F32)
    o_ref[...] = x_ref[...] + z1 * jax.nn.sigmoid(z2)


def _glu_res(y, w, layer, x):
    t, k = y.shape
    n = w.shape[2] // 2
    tm, tn = min(TM_MM, t), TN_MM
    nj = n // tn
    return pl.pallas_call(
        _glu_res_kernel,
        grid=(t // tm, nj),
        in_specs=[pl.BlockSpec((tm, k), lambda i, j: (i, 0)),
                  pl.BlockSpec((1, k, tn), lambda i, j: (layer, 0, j)),
                  pl.BlockSpec((1, k, tn), lambda i, j: (layer, 0, j + nj)),
                  pl.BlockSpec((tm, tn), lambda i, j: (i, j))],
        out_specs=pl.BlockSpec((tm, tn), lambda i, j: (i, j)),
        out_shape=jax.ShapeDtypeStruct((t, n), F32),
        compiler_params=_cparams(("parallel", "arbitrary")),
        name="glu_proj",
    )(y, w, w, x)


def _gate_weight_kernel(a_ref, b_ref, o_ref):
    o_ref[...] = jnp.dot(a_ref[0][:, 0:b_ref.shape[0]], b_ref[...],
                         precision=lax.Precision.HIGHEST, preferred_element_type=F32)


def _gate_weight(w_in, layer, w_gate):
    d = w_in.shape[1]
    return pl.pallas_call(
        _gate_weight_kernel,
        grid=(1,),
        in_specs=[pl.BlockSpec((1, d, LANES), lambda i: (layer, 0, OFF_LR // LANES)),
                  pl.BlockSpec(w_gate.shape, lambda i: (0, 0))],
        out_specs=pl.BlockSpec((d, w_gate.shape[1]), lambda i: (0, 0)),
        out_shape=jax.ShapeDtypeStruct((d, w_gate.shape[1]), F32),
        name="gate_weight",
    )(w_in, w_gate)


def _store_paired(dst_ref, v, bsz, ts):
    for c in range(v.shape[1] // LANES):
        for b in range(bsz):
            dst_ref.at[c // 2][pl.ds((c % 2) * bsz + b, ts, stride=SUBLANES), :] = (
                v[b * ts:(b + 1) * ts, c * LANES:(c + 1) * LANES])


def _load_paired(src_ref, bsz, ts, nchunks):
    cols = []
    for c in range(nchunks):
        cols.append(jnp.concatenate(
            [src_ref.at[c // 2][pl.ds((c % 2) * bsz + b, ts, stride=SUBLANES), :]
             for b in range(bsz)], axis=0))
    return jnp.concatenate(cols, axis=1)


def _step_rows(t0, k):
    base = pl.multiple_of(t0 * (SCAN_UNROLL * SUBLANES), SCAN_UNROLL * SUBLANES)
    return pl.ds(base + k * SUBLANES, SUBLANES)


def _pair_rows(v, batch):
    out = []
    for j in range(v.shape[1] // (2 * LANES)):
        lo = jnp.broadcast_to(v[:, (2 * j) * LANES:(2 * j + 1) * LANES], (batch, LANES))
        hi = jnp.broadcast_to(v[:, (2 * j + 1) * LANES:(2 * j + 2) * LANES], (batch, LANES))
        out.append(jnp.concatenate([lo, hi], axis=0))
    return out


def _lru_kernel(zx_ref, zg_ref, cw_ref, cb_ref, wa_ref, ba_ref, wx_ref, bx_ref, lam_ref,
                y_ref, xe_ref, a_ref, u_ref, h_ref):
    bsz, ts, c = zx_ref.shape
    rows = bsz * ts
    npair = c // (2 * LANES)
    nblk = wa_ref.shape[0]
    wblk = wa_ref.shape[1]

    @pl.when(pl.program_id(0) == 0)
    def _():
        xe_ref[:, 0:SUBLANES, :] = jnp.zeros((bsz, SUBLANES, c), F32)
        h_ref[...] = jnp.zeros(h_ref.shape, F32)

    x = zx_ref[...].astype(F32)
    xe_ref[:, SUBLANES:SUBLANES + ts, :] = x
    xc = cb_ref[...][None]
    for k in range(CONV_WIDTH):
        off = SUBLANES - (CONV_WIDTH - 1) + k
        xc = xc + cw_ref[k:k + 1, :][None] * xe_ref[:, off:off + ts, :]
    xe_ref[:, 0:SUBLANES, :] = x[:, ts - SUBLANES:, :]
    xc = xc.reshape(rows, c)

    xcb = xc.astype(BF16)
    ra, ia = [], []
    for b in range(nblk):
        xs = xcb[:, b * wblk:(b + 1) * wblk]
        ra.append(jnp.dot(xs, wa_ref[b], preferred_element_type=F32))
        ia.append(jnp.dot(xs, wx_ref[b], preferred_element_type=F32))
    r = jax.nn.sigmoid(jnp.concatenate(ra, axis=1) + ba_ref[...])
    i = jax.nn.sigmoid(jnp.concatenate(ia, axis=1) + bx_ref[...])
    log_a = (-LRU_C) * r * _softplus(-lam_ref[...])
    a = jnp.exp(log_a)
    mult = jnp.sqrt(1.0 - a * a)
    u = mult * (i * xc)
    _store_paired(a_ref, a, bsz, ts)
    _store_paired(u_ref, u, bsz, ts)

    def step(t0, hs):
        hs = list(hs)
        for k in range(SCAN_UNROLL):
            rs = _step_rows(t0, k)
            for j in range(npair):
                hs[j] = a_ref[j, rs, :] * hs[j] + u_ref[j, rs, :]
                u_ref[j, rs, :] = hs[j]
        return tuple(hs)

    hs = lax.fori_loop(0, ts // SCAN_UNROLL, step, tuple(h_ref[j] for j in range(npair)))
    for j in range(npair):
        h_ref[j] = hs[j]

    h = _load_paired(u_ref, bsz, ts, c // LANES)
    g = zg_ref[...].astype(F32).reshape(rows, c)
    y_ref[...] = (h * _gelu_tanh(g)).astype(y_ref.dtype).reshape(bsz, ts, c)


def _lru(z3, conv_w, conv_b, wa_bd, ba, wx_bd, bx, lam):
    bsz, s, _ = z3.shape
    assert 2 * bsz == SUBLANES
    ts = min(TS_LRU, s)
    c = D_LRU
    rows = bsz * ts
    full = lambda shape: pl.BlockSpec(shape, lambda i: (0,) * len(shape))
    return pl.pallas_call(
        _lru_kernel,
        grid=(s // ts,),
        in_specs=[pl.BlockSpec((bsz, ts, c), lambda i: (0, i, OFF_LRU_X // c)),
                  pl.BlockSpec((bsz, ts, c), lambda i: (0, i, OFF_LRU_G // c)),
                  full(conv_w.shape), full(conv_b.shape), full(wa_bd.shape), full(ba.shape),
                  full(wx_bd.shape), full(bx.shape), full(lam.shape)],
        out_specs=pl.BlockSpec((bsz, ts, c), lambda i: (0, i, 0)),
        out_shape=jax.ShapeDtypeStruct((bsz, s, c), BF16),
        scratch_shapes=[pltpu.VMEM((bsz, ts + SUBLANES, c), F32),
                        pltpu.VMEM((c // (2 * LANES), ts * SUBLANES, LANES), F32),
                        pltpu.VMEM((c // (2 * LANES), ts * SUBLANES, LANES), F32),
                        pltpu.VMEM((c // (2 * LANES), SUBLANES, LANES), F32)],
        compiler_params=_cparams(("arbitrary",)),
        name="rg_lru",
    )(z3, z3, conv_w, conv_b, wa_bd, ba, wx_bd, bx, lam)


def _gla_kernel(q_ref, k_ref, v_ref, og_ref, gl_ref, bg_ref, hn_ref, y_ref, st_ref):
    tc = q_ref.shape[1]
    cs = GLA_CHUNK

    @pl.when(pl.program_id(1) == 0)
    def _():
        st_ref[...] = jnp.zeros(st_ref.shape, F32)

    row = lax.broadcasted_iota(jnp.int32, (tc, tc), 0)
    col = lax.broadcasted_iota(jnp.int32, (tc, tc), 1)
    blk = jnp.where(row // cs == col // cs, 1.0, 0.0)
    tri = jnp.where(row >= col, blk, 0.0)
    causal = tri > 0.0
    for h in range(GLA_HEADS):
        dk = slice(h * GLA_DK, (h + 1) * GLA_DK)
        dv = slice(h * GLA_DV, (h + 1) * GLA_DV)
        x = gl_ref[0, :, dk].astype(F32) + bg_ref[:, dk]
        gk = (jnp.minimum(x, 0.0) - jnp.log1p(jnp.exp(-jnp.abs(x)))) * (1.0 / GLA_TAU)
        bcum = jnp.dot(tri, gk, precision=lax.Precision.HIGHEST, preferred_element_type=F32)
        btot = jnp.dot(blk, gk, precision=lax.Precision.HIGHEST, preferred_element_type=F32)
        q = q_ref[0, :, dk].astype(F32) * (GLA_DK ** -0.5)
        k = k_ref[0, :, dk].astype(F32)
        v = v_ref[0, :, dv]
        qs = (q * jnp.exp(bcum)).astype(BF16)
        ks = (k * jnp.exp(-bcum)).astype(BF16)
        ke = (k * jnp.exp(btot - bcum)).astype(BF16)
        att = lax.dot_general(qs, ks, (((1,), (1,)), ((), ())), preferred_element_type=F32)
        att = jnp.where(causal, att, 0.0).astype(BF16)
        o_intra = jnp.dot(att, v, preferred_element_type=F32)
        st = st_ref[h]
        outs = []
        for c in range(tc // cs):
            sl = slice(c * cs, (c + 1) * cs)
            outs.append(o_intra[sl] + lax.dot_general(
                qs[sl], st.astype(BF16), (((1,), (1,)), ((), ())), preferred_element_type=F32))
            ds = lax.dot_general(v[sl], ke[sl], (((0,), (0,)), ((), ())),
                                 preferred_element_type=F32)
            st = jnp.exp(btot[c * cs:c * cs + 1, :]) * st + ds
        st_ref[h] = st
        o = jnp.concatenate(outs, axis=0)
        o = o * lax.rsqrt(jnp.mean(o * o, axis=-1, keepdims=True) + EPS)
        y = (o * hn_ref[:, dv]) * _silu(og_ref[0, :, dv].astype(F32))
        y_ref[0, :, dv] = y.astype(y_ref.dtype)


def _gla(z3, b_gate, head_norm):
    bsz, s, _ = z3.shape
    tc = min(TC_GLA, s)
    qb, kb, vb, ob, gb = (OFF_Q // GLA_QK, OFF_K // GLA_QK, OFF_V // GLA_V, OFF_OG // GLA_V,
                          OFF_LR // GLA_QK)
    return pl.pallas_call(
        _gla_kernel,
        grid=(bsz, s // tc),
        in_specs=[pl.BlockSpec((1, tc, GLA_QK), lambda b, i: (b, i, qb)),
                  pl.BlockSpec((1, tc, GLA_QK), lambda b, i: (b, i, kb)),
                  pl.BlockSpec((1, tc, GLA_V), lambda b, i: (b, i, vb)),
                  pl.BlockSpec((1, tc, GLA_V), lambda b, i: (b, i, ob)),
                  pl.BlockSpec((1, tc, GLA_QK), lambda b, i: (b, i, gb)),
                  pl.BlockSpec((1, GLA_QK), lambda b, i: (0, 0)),
                  pl.BlockSpec((1, GLA_V), lambda b, i: (0, 0))],
        out_specs=pl.BlockSpec((1, tc, GLA_V), lambda b, i: (b, i, 0)),
        out_shape=jax.ShapeDtypeStruct((bsz, s, GLA_V), BF16),
        scratch_shapes=[pltpu.VMEM((GLA_HEADS, GLA_DV, GLA_DK), F32)],
        compiler_params=_cparams(("parallel", "arbitrary")),
        name="gla",
    )(z3, z3, z3, z3, z3, b_gate, head_norm)


def _s5_kernel(u_ref, lre_ref, lim_ref, lst_ref, bre_ref, bim_ref, cre_ref, cim_ref, d_ref,
               y_ref, wbr_ref, wbi_ref, ar_ref, ai_ref, sr_ref, si_ref, hr_ref, hi_ref):
    bsz, ts, cb = u_ref.shape
    rows = bsz * ts
    nst = lre_ref.shape[2]
    npair = nst // (2 * LANES)

    @pl.when(pl.program_id(1) == 0)
    def _():
        lr = jnp.minimum(lre_ref[0], S5_MAX_RE)
        li = lim_ref[0]
        dt = jnp.exp(lst_ref[0])
        mag = jnp.exp(lr * dt)
        ab_re = mag * jnp.cos(li * dt)
        ab_im = mag * jnp.sin(li * dt)
        den = lr * lr + li * li
        coef_re = ((ab_re - 1.0) * lr + ab_im * li) / den
        coef_im = (ab_im * lr - (ab_re - 1.0) * li) / den
        bre = bre_ref[0]
        bim = bim_ref[0]
        wbr_ref[...] = (coef_re * bre - coef_im * bim).astype(BF16)
        wbi_ref[...] = (coef_re * bim + coef_im * bre).astype(BF16)
        for j, v in enumerate(_pair_rows(ab_re, bsz)):
            ar_ref[j] = v
        for j, v in enumerate(_pair_rows(ab_im, bsz)):
            ai_ref[j] = v
        hr_ref[...] = jnp.zeros(hr_ref.shape, F32)
        hi_ref[...] = jnp.zeros(hi_ref.shape, F32)

    u = u_ref[...].reshape(rows, cb)
    _store_paired(sr_ref, jnp.dot(u, wbr_ref[...], preferred_element_type=F32), bsz, ts)
    _store_paired(si_ref, jnp.dot(u, wbi_ref[...], preferred_element_type=F32), bsz, ts)

    ars = [ar_ref[j] for j in range(npair)]
    ais = [ai_ref[j] for j in range(npair)]

    def step(t0, carry):
        hr, hi = list(carry[0]), list(carry[1])
        for k in range(SCAN_UNROLL):
            rs = _step_rows(t0, k)
            for j in range(npair):
                nr = ars[j] * hr[j] - ais[j] * hi[j] + sr_ref[j, rs, :]
                ni = ars[j] * hi[j] + ais[j] * hr[j] + si_ref[j, rs, :]
                hr[j], hi[j] = nr, ni
                sr_ref[j, rs, :] = nr
                si_ref[j, rs, :] = ni
        return tuple(hr), tuple(hi)

    init = (tuple(hr_ref[j] for j in range(npair)), tuple(hi_ref[j] for j in range(npair)))
    hr, hi = lax.fori_loop(0, ts // SCAN_UNROLL, step, init)
    for j in range(npair):
        hr_ref[j] = hr[j]
        hi_ref[j] = hi[j]

    h_re = _load_paired(sr_ref, bsz, ts, nst // LANES).astype(BF16)
    h_im = _load_paired(si_ref, bsz, ts, nst // LANES).astype(BF16)
    y = jnp.dot(h_re, cre_ref[0], preferred_element_type=F32)
    y = y - jnp.dot(h_im, cim_ref[0], preferred_element_type=F32)
    y = y + d_ref[...] * u.astype(F32)
    y_ref[...] = _gelu_tanh(y).astype(y_ref.dtype).reshape(bsz, ts, cb)


def _s5(u3, lre, lim, lst, bre_bd, bim_bd, cre_bd, cim_bd, d_skip):
    bsz, s, c = u3.shape
    assert 2 * bsz == SUBLANES
    ts = min(TS_S5, s)
    cb = S5_CB
    ncb = c // cb
    nst = lre.shape[2]
    rows = bsz * ts
    npair = nst // (2 * LANES)
    blk = lambda shape: pl.BlockSpec((1,) + shape, lambda ci, i: (ci, 0, 0))
    return pl.pallas_call(
        _s5_kernel,
        grid=(ncb, s // ts),
        in_specs=[pl.BlockSpec((bsz, ts, cb), lambda ci, i: (0, i, ci)),
                  blk((1, nst)), blk((1, nst)), blk((1, nst)),
                  blk((cb, nst)), blk((cb, nst)), blk((nst, cb)), blk((nst, cb)),
                  pl.BlockSpec((1, cb), lambda ci, i: (0, ci))],
        out_specs=pl.BlockSpec((bsz, ts, cb), lambda ci, i: (0, i, ci)),
        out_shape=jax.ShapeDtypeStruct((bsz, s, c), BF16),
        scratch_shapes=[pltpu.VMEM((cb, nst), BF16), pltpu.VMEM((cb, nst), BF16),
                        pltpu.VMEM((npair, SUBLANES, LANES), F32),
                        pltpu.VMEM((npair, SUBLANES, LANES), F32),
                        pltpu.VMEM((npair, ts * SUBLANES, LANES), F32),
                        pltpu.VMEM((npair, ts * SUBLANES, LANES), F32),
                        pltpu.VMEM((npair, SUBLANES, LANES), F32),
                        pltpu.VMEM((npair, SUBLANES, LANES), F32)],
        compiler_params=_cparams(("parallel", "arbitrary")),
        name="s5",
    )(u3, lre, lim, lst, bre_bd, bim_bd, cre_bd, cim_bd, d_skip)


ROUTE_GROUP_LANE0 = 0
ROUTE_EXPERT_LANE0 = N_EXPERT_GROUPS
RI_E0, RI_E1, RI_C0, RI_C1, RI_R0, RI_R1 = 0, 1, 2, 3, 4, 5


def _router_kernel(x_ref, g_ref, w_ref, b_ref, ri_ref, rt_ref, cnt_ref, run_ref, wh_ref, wl_ref):
    tm = x_ref.shape[0]
    neg = -jnp.inf

    @pl.when(pl.program_id(0) == 0)
    def _():
        run_ref[...] = jnp.zeros(run_ref.shape, F32)
        w = w_ref[...]
        wh = w.astype(BF16)
        wh_ref[...] = wh
        wl_ref[...] = (w - wh.astype(F32)).astype(BF16)

    xn = _rms(x_ref[...], g_ref[...])
    xh = xn.astype(BF16)
    xl = (xn - xh.astype(F32)).astype(BF16)
    logits = (jnp.dot(xh, wh_ref[...], preferred_element_type=F32)
              + jnp.dot(xl, wh_ref[...], preferred_element_type=F32)
              + jnp.dot(xh, wl_ref[...], preferred_element_type=F32)) + b_ref[...]
    lane = lax.broadcasted_iota(jnp.int32, (tm, LANES), 1)

    def first_lane(mask):
        return jnp.min(jnp.where(mask, lane, LANES), axis=1, keepdims=True)

    lg = jnp.where(lane < N_EXPERT_GROUPS, logits, neg)
    mg = jnp.max(lg, axis=1, keepdims=True)
    sg = jnp.sum(jnp.exp(lg - mg), axis=1, keepdims=True)
    gate_g = 1.0 / sg
    g_idx = first_lane(lg == mg)
    lo = ROUTE_EXPERT_LANE0 + EXPERTS_PER_GROUP * g_idx
    in_group = jnp.abs(2 * (lane - lo) - (EXPERTS_PER_GROUP - 1)) < EXPERTS_PER_GROUP
    le = jnp.where(in_group, logits, neg)
    m1 = jnp.max(le, axis=1, keepdims=True)
    i1 = first_lane(le == m1)
    le2 = jnp.where(lane == i1, neg, le)
    m2 = jnp.max(le2, axis=1, keepdims=True)
    i2 = first_lane(le2 == m2)
    se = jnp.sum(jnp.exp(le - m1), axis=1, keepdims=True)
    p1 = 1.0 / se
    p2 = jnp.exp(m2 - m1) / se
    c0 = gate_g * (p1 / (p1 + p2))
    c1 = gate_g * (p2 / (p1 + p2))

    sel0 = lane == i1
    sel1 = lane == i2
    onehot = jnp.where(sel0, 1.0, jnp.where(sel1, 1.0, 0.0))
    r_i = lax.broadcasted_iota(jnp.int32, (tm, tm), 0)
    c_i = lax.broadcasted_iota(jnp.int32, (tm, tm), 1)
    before = jnp.where(c_i < r_i, 1.0, 0.0).astype(BF16)
    prefix = jnp.dot(before, onehot.astype(BF16), preferred_element_type=F32) + run_ref[...]
    rank0 = jnp.sum(jnp.where(sel0, prefix, 0.0), axis=1, keepdims=True)
    rank1 = jnp.sum(jnp.where(sel1, prefix, 0.0), axis=1, keepdims=True)
    run_ref[...] = run_ref[...] + jnp.sum(onehot, axis=0, keepdims=True)
    cnt_ref[...] = run_ref[...]

    e0 = (i1 - ROUTE_EXPERT_LANE0).astype(F32)
    e1 = (i2 - ROUTE_EXPERT_LANE0).astype(F32)
    rec = jnp.zeros((tm, LANES), F32)
    for ln, val in ((RI_E0, e0), (RI_E1, e1), (RI_C0, c0), (RI_C1, c1), (RI_R0, rank0),
                    (RI_R1, rank1)):
        rec = jnp.where(lane == ln, val, rec)
    ri_ref[...] = rec
    rt_ref[...] = rec.T[0:SUBLANES, :]


def _router(x, g, w_route, b_route):
    t, d = x.shape
    tm = min(TM_ROUTE, t)
    return pl.pallas_call(
        _router_kernel,
        grid=(t // tm,),
        in_specs=[pl.BlockSpec((tm, d), lambda i: (i, 0)),
                  pl.BlockSpec((1, d), lambda i: (0, 0)),
                  pl.BlockSpec((d, LANES), lambda i: (0, 0)),
                  pl.BlockSpec((1, LANES), lambda i: (0, 0))],
        out_specs=[pl.BlockSpec((tm, LANES), lambda i: (i, 0)),
                   pl.BlockSpec((SUBLANES, tm), lambda i: (0, i)),
                   pl.BlockSpec((1, LANES), lambda i: (0, 0))],
        out_shape=[jax.ShapeDtypeStruct((t, LANES), F32),
                   jax.ShapeDtypeStruct((SUBLANES, t), F32),
                   jax.ShapeDtypeStruct((1, LANES), F32)],
        scratch_shapes=[pltpu.VMEM((1, LANES), F32), pltpu.VMEM((d, LANES), BF16),
                        pltpu.VMEM((d, LANES), BF16)],
        compiler_params=_cparams(("arbitrary",)),
        name="router",
    )(x, g, w_route, b_route)


def _piece_groups(n_rows, weights):
    total = sum(weights)
    bounds = [round(n_rows * sum(weights[:k]) / total) for k in range(len(weights) + 1)]
    return [range(bounds[k], bounds[k + 1]) for k in range(len(weights))]


def _dispatch_kernel(n_rows, pos_ref, zrow_ref, x_ref, g_ref, xs_hbm, xbuf, zbuf, sem, zsem):
    i = pl.program_id(0)
    nb = pl.num_programs(0)
    tm, d = x_ref.shape
    t = nb * tm
    tme = zbuf.shape[0]
    slot = i % DISPATCH_BUFS

    def scatter_wait(s):
        for k in range(2):
            pltpu.make_async_copy(xbuf.at[s], xs_hbm.at[pl.ds(0, tm), :], sem.at[s]).wait()

    @pl.when(i == 0)
    def _():
        zbuf[...] = jnp.zeros(zbuf.shape, F32)

        def fill(e):
            row = pl.multiple_of(jnp.maximum(zrow_ref[e], 0), tme)
            return pltpu.make_async_copy(zbuf, xs_hbm.at[pl.ds(row, tme), :], zsem)

        for e in range(n_rows):
            @pl.when(zrow_ref[e] >= 0)
            def _():
                fill(e).start()

        for e in range(n_rows):
            @pl.when(zrow_ref[e] >= 0)
            def _():
                fill(e).wait()

    @pl.when(i >= DISPATCH_BUFS)
    def _():
        scatter_wait(slot)

    xbuf[slot] = _rms(x_ref[...], g_ref[...])
    for r in range(tm):
        for k in range(2):
            dst = pos_ref[k * t + i * tm + r]
            pltpu.make_async_copy(xbuf.at[slot, pl.ds(r, 1), :], xs_hbm.at[pl.ds(dst, 1), :],
                                  sem.at[slot]).start(priority=k)

    @pl.when(i == nb - 1)
    def _():
        for a in range(min(DISPATCH_BUFS, nb)):
            scatter_wait((i - a) % DISPATCH_BUFS)


def _dispatch(x, g, pos_flat, zero_rows, n_slots, tme):
    t, d = x.shape
    tm = min(TM_DISPATCH, t)
    n_e = zero_rows.shape[0]
    return pl.pallas_call(
        functools.partial(_dispatch_kernel, n_e),
        grid_spec=pltpu.PrefetchScalarGridSpec(
            num_scalar_prefetch=2,
            grid=(t // tm,),
            in_specs=[pl.BlockSpec((tm, d), lambda i, *_: (i, 0)),
                      pl.BlockSpec((1, d), lambda i, *_: (0, 0))],
            out_specs=pl.BlockSpec(memory_space=pl.ANY),
            scratch_shapes=[pltpu.VMEM((DISPATCH_BUFS, tm, d), F32),
                            pltpu.VMEM((tme, d), F32),
                            pltpu.SemaphoreType.DMA((DISPATCH_BUFS,)),
                            pltpu.SemaphoreType.DMA(())]),
        out_shape=jax.ShapeDtypeStruct((n_slots, d), F32),
        compiler_params=_cparams(("arbitrary",)),
        name="dispatch",
    )(pos_flat, zero_rows, x, g)


def _experts_kernel(layer, eidx_ref, eseq_ref, meta_ref, x_ref, w1_hbm, w3_hbm, w2_hbm, y_ref,
                    ws1, ws3, ws2, w1b_ref, w3b_ref, w2b_ref, wsem):
    i = pl.program_id(0)
    nv = meta_ref[0]
    ne = meta_ref[1]

    def weight_copies(k):
        e = eseq_ref[k]
        s = k % 2
        return [pltpu.make_async_copy(w_hbm.at[layer, e], ws.at[s], wsem.at[s])
                for w_hbm, ws in ((w1_hbm, ws1), (w3_hbm, ws3), (w2_hbm, ws2))]

    @pl.when(i == 0)
    def _():
        for c in weight_copies(0):
            c.start(priority=WEIGHT_DMA_PRIORITY)

        @pl.when(ne > 1)
        def _():
            for c in weight_copies(1):
                c.start(priority=WEIGHT_DMA_PRIORITY)

    @pl.when(i >= nv)
    def _():
        y_ref[...] = jnp.zeros(y_ref.shape, y_ref.dtype)

    @pl.when(i < nv)
    def _():
        k = eidx_ref[i]

        @pl.when(jnp.logical_or(i == 0, k != eidx_ref[jnp.maximum(i - 1, 0)]))
        def _():
            for c in weight_copies(k):
                c.wait()
            s = k % 2
            w1b_ref[...] = ws1[s].astype(BF16)
            w3b_ref[...] = ws3[s].astype(BF16)
            w2b_ref[...] = ws2[s].astype(BF16)

            @pl.when(k + 2 < ne)
            def _():
                for c in weight_copies(k + 2):
                    c.start(priority=WEIGHT_DMA_PRIORITY)

        xb = x_ref[...].astype(BF16)
        h = _silu(jnp.dot(xb, w1b_ref[...], preferred_element_type=F32))
        h = h * jnp.dot(xb, w3b_ref[...], preferred_element_type=F32)
        y_ref[...] = jnp.dot(h.astype(BF16), w2b_ref[...], preferred_element_type=F32)


def _experts(tile_eidx, expert_seq, meta, xs, w1, w3, w2, layer):
    d, f = w1.shape[-2:]
    nt = tile_eidx.shape[0]
    tme = xs.shape[0] // nt
    hbm = pl.BlockSpec(memory_space=pl.ANY)
    return pl.pallas_call(
        functools.partial(_experts_kernel, layer),
        grid_spec=pltpu.PrefetchScalarGridSpec(
            num_scalar_prefetch=3,
            grid=(nt,),
            in_specs=[pl.BlockSpec((tme, d), lambda i, eidx, eseq, m: (jnp.minimum(i, m[0] - 1), 0)),
                      hbm, hbm, hbm],
            out_specs=pl.BlockSpec((tme, d), lambda i, *_: (i, 0)),
            scratch_shapes=[pltpu.VMEM((2, d, f), F32), pltpu.VMEM((2, d, f), F32),
                            pltpu.VMEM((2, f, d), F32),
                            pltpu.VMEM((d, f), BF16), pltpu.VMEM((d, f), BF16),
                            pltpu.VMEM((f, d), BF16),
                            pltpu.SemaphoreType.DMA((2,))]),
        out_shape=jax.ShapeDtypeStruct((nt * tme, d), F32),
        compiler_params=_cparams(("arbitrary",)),
        name="experts",
    )(tile_eidx, expert_seq, meta, xs, w1, w3, w2)


def _route_plan(rt, counts, tme):
    t = rt.shape[1]
    nt = 2 * t // tme + N_EXPERTS
    e = rt[RI_E0:RI_E1 + 1].astype(jnp.int32)
    rank = rt[RI_R0:RI_R1 + 1].astype(jnp.int32)
    cnt = counts[0, ROUTE_EXPERT_LANE0:ROUTE_EXPERT_LANE0 + N_EXPERTS].astype(jnp.int32)
    padded = ((cnt + tme - 1) // tme) * tme
    ends = jnp.cumsum(padded)
    offs = ends - padded
    experts = jnp.arange(N_EXPERTS, dtype=jnp.int32)[:, None, None]
    pos = jnp.sum(jnp.where(e[None] == experts, offs[:, None, None], 0), axis=0) + rank
    n_valid = (ends[-1] // tme).astype(jnp.int32)
    starts = jnp.arange(nt, dtype=jnp.int32) * tme
    tile_expert = jnp.sum((starts[:, None] >= ends[None, :]).astype(jnp.int32), axis=1)
    last = jnp.take(tile_expert, jnp.maximum(n_valid - 1, 0))
    tile_expert = jnp.where(jnp.arange(nt) < n_valid, tile_expert, last)
    tile_expert = jnp.minimum(tile_expert, N_EXPERTS - 1).astype(jnp.int32)
    used_cum = jnp.cumsum((cnt > 0).astype(jnp.int32))
    n_used = used_cum[-1]
    ordinals = jnp.arange(N_EXPERTS, dtype=jnp.int32)
    expert_seq = jnp.sum((used_cum[None, :] <= ordinals[:, None]).astype(jnp.int32), axis=1)
    expert_seq = jnp.minimum(expert_seq, N_EXPERTS - 1).astype(jnp.int32)
    tile_eidx = (jnp.take(used_cum, tile_expert) - 1).astype(jnp.int32)
    meta = jnp.stack([n_valid, n_used]).astype(jnp.int32)
    pos_flat = pos.reshape(-1)
    zero_rows = jnp.where(cnt > 0, ends - tme, -1).astype(jnp.int32)
    return tile_eidx, expert_seq, meta, zero_rows, pos_flat, nt


def _ple_kernel(final, pos_ref, ys_hbm, x_ref, ri_ref, g_ref, wg_ref, bg_ref, p_ref, wp_ref,
                gf_ref, o_ref, ybuf, sem):
    i = pl.program_id(0)
    nb = pl.num_programs(0)
    tm, d = o_ref.shape
    t = nb * tm
    slot = i % PLE_ROW_BUFS

    def gather_row(blk, r, s):
        for k in range(2):
            src = pos_ref[k * t + blk * tm + r]
            pltpu.make_async_copy(ys_hbm.at[pl.ds(src, 1), :], ybuf.at[s, k, pl.ds(r, 1), :],
                                  sem.at[s]).start(priority=k)

    def gather_block(blk, s):
        def body(r0, c):
            for k in range(DMA_UNROLL):
                gather_row(blk, r0 * DMA_UNROLL + k, s)
            return c

        lax.fori_loop(0, tm // DMA_UNROLL, body, 0)

    def gather_wait(s):
        for k in range(2):
            pltpu.make_async_copy(ys_hbm.at[pl.ds(0, tm), :], ybuf.at[s, k], sem.at[s]).wait()

    @pl.when(i == 0)
    def _():
        gather_block(0, 0)
        gather_block(jnp.minimum(1, nb - 1), 1)

    gather_wait(slot)
    nxt = jnp.minimum(i + 2, nb - 1)
    other = (i + 2) % PLE_ROW_BUFS
    groups = _piece_groups(tm, [1] * PLE_PIECES)
    ri = ri_ref[...]
    x2 = x_ref[...] + ri[:, RI_C0:RI_C0 + 1] * ybuf[slot, 0] + ri[:, RI_C1:RI_C1 + 1] * ybuf[slot, 1]
    o_ref[...] = x2
    xn = _rms(x2, g_ref[...]).astype(BF16)
    pb = p_ref[0].astype(BF16)
    w = d // PLE_PIECES
    for c in range(PLE_PIECES):
        cs = slice(c * w, (c + 1) * w)
        gate = jax.nn.sigmoid(jnp.dot(xn, wg_ref[0, :, cs], preferred_element_type=F32)
                              + bg_ref[:, cs])
        proj = jnp.dot(pb, wp_ref[0, :, cs], preferred_element_type=F32)
        o_ref[:, cs] = o_ref[:, cs] + gate * proj
        for r in groups[c]:
            gather_row(nxt, r, other)
    if final:
        o_ref[...] = _rms(o_ref[...], gf_ref[...])

    @pl.when(i == nb - 1)
    def _():
        gather_wait((i + 1) % PLE_ROW_BUFS)
        gather_wait(other)


def _ple(x, ys, pos_flat, rinfo, g, wg, bg, p, wp, layer, g_final):
    t, d = x.shape
    tm = min(TM_PLE, t)
    dp = p.shape[-1]
    final = layer == p.shape[0] - 1
    return pl.pallas_call(
        functools.partial(_ple_kernel, final),
        grid_spec=pltpu.PrefetchScalarGridSpec(
            num_scalar_prefetch=1,
            grid=(t // tm,),
            in_specs=[pl.BlockSpec(memory_space=pl.ANY),
                      pl.BlockSpec((tm, d), lambda i, pos: (i, 0)),
                      pl.BlockSpec((tm, LANES), lambda i, pos: (i, 0)),
                      pl.BlockSpec((1, d), lambda i, pos: (0, 0)),
                      pl.BlockSpec((1, d, d), lambda i, pos: (layer, 0, 0)),
                      pl.BlockSpec((1, d), lambda i, pos: (0, 0)),
                      pl.BlockSpec((1, tm, dp), lambda i, pos: (layer, i, 0)),
                      pl.BlockSpec((1, dp, d), lambda i, pos: (layer, 0, 0)),
                      pl.BlockSpec((1, d), lambda i, pos: (0, 0))],
            out_specs=pl.BlockSpec((tm, d), lambda i, pos: (i, 0)),
            scratch_shapes=[pltpu.VMEM((PLE_ROW_BUFS, 2, tm, d), F32),
                            pltpu.SemaphoreType.DMA((PLE_ROW_BUFS,))]),
        out_shape=jax.ShapeDtypeStruct((t, d), F32),
        compiler_params=_cparams(("arbitrary",)),
        name="ple",
    )(pos_flat, ys, x, rinfo, g, wg, bg, p, wp, g_final)


def _moe_ple(x, layer, g_ffn, w_rg, b_rg, w_re, b_re, w1, w3, w2, g_ple, wg, bg, p, wp,
             g_final):
    d = x.shape[1]
    pad = LANES - N_EXPERT_GROUPS - N_EXPERTS
    w_route = jnp.concatenate([w_rg, w_re, jnp.zeros((d, pad), F32)], axis=1)
    b_route = jnp.concatenate([b_rg, b_re, jnp.zeros((pad,), F32)])[None]
    rinfo, rt, counts = _router(x, g_ffn[None], w_route, b_route)
    tile_eidx, expert_seq, meta, zero_rows, pos_flat, nt = _route_plan(rt, counts, TM_EXP)
    xs = _dispatch(x, g_ffn[None], pos_flat, zero_rows, nt * TM_EXP, TM_EXP)
    ys = _experts(tile_eidx, expert_seq, meta, xs, w1, w3, w2, layer)
    return _ple(x, ys, pos_flat, rinfo, g_ple[None], wg, bg[None], p, wp, layer, g_final[None])


def _block_diag(w, nb):
    h, d, _ = w.shape
    w4 = w.reshape(h // nb, nb, d, d)
    out = jnp.einsum('cadk,ab->cadbk', w4, jnp.eye(nb, dtype=w.dtype))
    return out.reshape(h // nb, nb * d, nb * d)


def _even_layer(x, bsz, e, g, w_in, conv_w, conv_b, w_a, b_a, w_x, b_x, lam, w_gate, b_gate,
                head_norm, w_out):
    t = x.shape[0]
    w_gk = _gate_weight(w_in, e, w_gate)
    z = _norm_matmul(x, g[None], w_in, e, OFF_LR, w_gk)
    z3 = z.reshape(bsz, t // bsz, D_EVEN_Z)
    heads_per_blk = 2 * LANES // LRU_HEAD_DIM
    y_lru = _lru(z3, conv_w, conv_b[None], _block_diag(w_a, heads_per_blk).astype(BF16),
                 b_a[None], _block_diag(w_x, heads_per_blk).astype(BF16), b_x[None], lam[None])
    y_gla = _gla(z3, b_gate[None], head_norm[None])
    return _mm2_res(y_lru.reshape(t, D_LRU), y_gla.reshape(t, GLA_V), w_out, e, x)


def _odd_layer(x, bsz, o, g, w_in, lam_re, lam_im, log_step, b_re, b_im, c_re, c_im, d_skip,
               w_glu):
    t = x.shape[0]
    u = _norm_matmul(x, g[None], w_in, o, D_S5)
    gpb = S5_CB // S5_GROUP
    ncb = S5_GROUPS // gpb
    nst = gpb * S5_STATE
    eye = jnp.eye(gpb, dtype=F32)
    lay = lambda a: a.reshape(ncb, 1, nst)
    lst = jnp.repeat(log_step, S5_STATE)
    bexp = lambda b: jnp.einsum('cgph,gk->cghkp', b.reshape(ncb, gpb, S5_STATE, S5_GROUP),
                                eye).reshape(ncb, S5_CB, nst)
    cexp = lambda c: jnp.einsum('cgop,gk->cgpko', c.reshape(ncb, gpb, S5_GROUP, S5_STATE),
                                eye).reshape(ncb, nst, S5_CB).astype(BF16)
    y = _s5(u.reshape(bsz, t // bsz, D_S5), lay(lam_re), lay(lam_im), lay(lst),
            bexp(b_re), bexp(b_im), cexp(c_re), cexp(c_im), d_skip[None])
    return _glu_res(y.reshape(t, D_S5), w_glu, o, x)


def kernel(x, p, norm_mix, norm_ffn, norm_ple, norm_final, ev_w_in, lru_conv_w, lru_conv_b, lru_w_a, lru_b_a, lru_w_x, lru_b_x, lru_lambda, gla_w_gate, gla_b_gate, gla_norm, ev_w_out, od_w_in, s5_lambda_re, s5_lambda_im, s5_log_step, s5_b_re, s5_b_im, s5_c_re, s5_c_im, s5_d, od_w_glu, moe_w_router_group, moe_b_router_group, moe_w_router_expert, moe_b_router_expert, moe_w1, moe_w3, moe_w2, ple_w_gate, ple_b_gate, ple_w_proj):
    bsz, s, d = x.shape
    t = bsz * s
    depth = p.shape[0]
    h = x.reshape(t, d)
    p3 = p.reshape(depth, t, p.shape[-1])
    wg_bf = ple_w_gate.astype(BF16)
    wp_bf = ple_w_proj.astype(BF16)
    for l in range(depth):
        if l % 2 == 0:
            e = l // 2
            h = _even_layer(h, bsz, e, norm_mix[l], ev_w_in, lru_conv_w[e], lru_conv_b[e],
                            lru_w_a[e], lru_b_a[e], lru_w_x[e], lru_b_x[e], lru_lambda[e],
                            gla_w_gate[e], gla_b_gate[e], gla_norm[e], ev_w_out)
        else:
            o = l // 2
            h = _odd_layer(h, bsz, o, norm_mix[l], od_w_in, s5_lambda_re[o], s5_lambda_im[o],
                           s5_log_step[o], s5_b_re[o], s5_b_im[o], s5_c_re[o], s5_c_im[o],
                           s5_d[o], od_w_glu)
        h = _moe_ple(h, l, norm_ffn[l], moe_w_router_group[l], moe_b_router_group[l],
                     moe_w_router_expert[l], moe_b_router_expert[l], moe_w1, moe_w3,
                     moe_w2, norm_ple[l], wg_bf, ple_b_gate[l], p3, wp_bf, norm_final)
    return h.reshape(bsz, s, d)
```

```python
import functools
import math

import jax
import jax.numpy as jnp
from jax import lax
from jax.experimental import pallas as pl
from jax.experimental.pallas import tpu as pltpu

F32 = jnp.float32
BF16 = jnp.bfloat16

D_MODEL = 2048
D_LRU = 1024
LRU_HEADS = 16
LRU_HEAD_DIM = D_LRU // LRU_HEADS
CONV_WIDTH = 4
LRU_C = 8.0
GLA_HEADS = 4
GLA_DK = 128
GLA_DV = 256
GLA_QK = GLA_HEADS * GLA_DK
GLA_V = GLA_HEADS * GLA_DV
GLA_RANK = 16
GLA_TAU = 16.0
GLA_CHUNK = 64
OFF_LRU_X = 0
OFF_LRU_G = OFF_LRU_X + D_LRU
OFF_Q = OFF_LRU_G + D_LRU
OFF_K = OFF_Q + GLA_QK
OFF_V = OFF_K + GLA_QK
OFF_OG = OFF_V + GLA_V
OFF_LR = OFF_OG + GLA_V
D_EVEN_Z = OFF_LR + GLA_QK
D_S5 = 1024
S5_GROUP = 16
S5_GROUPS = D_S5 // S5_GROUP
S5_STATE = 64
S5_MAX_RE = -1e-4
N_EXPERT_GROUPS = 4
EXPERTS_PER_GROUP = 8
N_EXPERTS = N_EXPERT_GROUPS * EXPERTS_PER_GROUP
D_EXPERT = 512
D_PLE = 256
EPS = 1e-6

LANES = 128
SUBLANES = 8
VMEM_LIMIT = 56 * 1024 * 1024

TM_MM = 1024
TM_RES = 2048
TN_MM = 512
TS_LRU = 256
TC_GLA = 256
TS_S5 = 256
S5_CB = 256
TM_ROUTE = 512
TM_EXP = 256
TM_PLE = 256
PLE_PIECES = 4
PLE_ROW_BUFS = 3
TM_DISPATCH = 256
DISPATCH_BUFS = 3
WEIGHT_DMA_PRIORITY = 1
SCAN_UNROLL = 8
DMA_UNROLL = 8


def _cparams(sem):
    return pltpu.CompilerParams(dimension_semantics=sem, vmem_limit_bytes=VMEM_LIMIT)


def _rms(x, g):
    return x * lax.rsqrt(jnp.mean(x * x, axis=-1, keepdims=True) + EPS) * g


def _gelu_tanh(x):
    c = math.sqrt(2.0 / math.pi)
    return 0.5 * x * (1.0 + jnp.tanh(c * (x + 0.044715 * (x * x * x))))


def _softplus(x):
    return jnp.maximum(x, 0.0) + jnp.log1p(jnp.exp(-jnp.abs(x)))


def _silu(x):
    return x * jax.nn.sigmoid(x)


def _norm_matmul_kernel(nj, x_ref, g_ref, w_ref, *rest):
    o_ref, xn_ref = rest[-2:]
    j = pl.program_id(1)

    @pl.when(j == 0)
    def _():
        xn_ref[...] = _rms(x_ref[...], g_ref[...]).astype(BF16)

    @pl.when(j < nj)
    def _():
        o_ref[...] = jnp.dot(xn_ref[...], w_ref[...].astype(BF16),
                             preferred_element_type=F32).astype(o_ref.dtype)

    if len(rest) == 3:
        @pl.when(j >= nj)
        def _():
            o_ref[...] = jnp.dot(xn_ref[...], rest[0][...].astype(BF16),
                                 preferred_element_type=F32).astype(o_ref.dtype)


def _norm_matmul(x, g, w, layer, n_main, w_extra=None):
    t, d = x.shape
    tm, tn = min(TM_MM, t), TN_MM
    nj = n_main // tn
    in_specs = [pl.BlockSpec((tm, d), lambda i, j: (i, 0)),
                pl.BlockSpec((1, d), lambda i, j: (0, 0)),
                pl.BlockSpec((d, tn), lambda i, j: (layer, jnp.minimum(j, nj - 1)))]
    args = [x, g, w.reshape(-1, w.shape[-1])]
    n_extra = 0
    if w_extra is not None:
        n_extra = 1
        in_specs.append(pl.BlockSpec((d, tn), lambda i, j: (0, 0)))
        args.append(w_extra)
    return pl.pallas_call(
        functools.partial(_norm_matmul_kernel, nj),
        grid=(t // tm, nj + n_extra),
        in_specs=in_specs,
        out_specs=pl.BlockSpec((tm, tn), lambda i, j: (i, j)),
        out_shape=jax.ShapeDtypeStruct((t, (nj + n_extra) * tn), BF16),
        scratch_shapes=[pltpu.VMEM((tm, d), BF16)],
        compiler_params=_cparams(("parallel", "arbitrary")),
        name="norm_matmul",
    )(*args)


def _mm2_res_kernel(a1_ref, a2_ref, w_ref, x_ref, o_ref):
    k1 = a1_ref.shape[1]
    acc = jnp.dot(a1_ref[...], w_ref[0, 0:k1, :].astype(BF16), preferred_element_type=F32)
    acc = acc + jnp.dot(a2_ref[...], w_ref[0, k1:, :].astype(BF16), preferred_element_type=F32)
    o_ref[...] = x_ref[...] + acc


def _mm2_res(a1, a2, w, layer, x):
    t, k1 = a1.shape
    k2 = a2.shape[1]
    n = w.shape[2]
    tm, tn = min(TM_RES, t), TN_MM
    return pl.pallas_call(
        _mm2_res_kernel,
        grid=(t // tm, n // tn),
        in_specs=[pl.BlockSpec((tm, k1), lambda i, j: (i, 0)),
                  pl.BlockSpec((tm, k2), lambda i, j: (i, 0)),
                  pl.BlockSpec((1, k1 + k2, tn), lambda i, j: (layer, 0, j)),
                  pl.BlockSpec((tm, tn), lambda i, j: (i, j))],
        out_specs=pl.BlockSpec((tm, tn), lambda i, j: (i, j)),
        out_shape=jax.ShapeDtypeStruct((t, n), F32),
        compiler_params=_cparams(("parallel", "arbitrary")),
        name="out_proj",
    )(a1, a2, w, x)


def _glu_res_kernel(y_ref, w1_ref, w2_ref, x_ref, o_ref):
    y = y_ref[...]
    z1 = jnp.dot(y, w1_ref[0].astype(BF16), preferred_element_type=F32)
    z2 = jnp.dot(y, w2_ref[0].astype(BF16), preferred_element_type=F32)
    o_ref[...] = x_ref[...] + z1 * jax.nn.sigmoid(z2)


def _glu_res(y, w, layer, x):
    t, k = y.shape
    n = w.shape[2] // 2
    tm, tn = min(TM_RES, t), TN_MM
    nj = n // tn
    return pl.pallas_call(
        _glu_res_kernel,
        grid=(t // tm, nj),
        in_specs=[pl.BlockSpec((tm, k), lambda i, j: (i, 0)),
                  pl.BlockSpec((1, k, tn), lambda i, j: (layer, 0, j)),
                  pl.BlockSpec((1, k, tn), lambda i, j: (layer, 0, j + nj)),
                  pl.BlockSpec((tm, tn), lambda i, j: (i, j))],
        out_specs=pl.BlockSpec((tm, tn), lambda i, j: (i, j)),
        out_shape=jax.ShapeDtypeStruct((t, n), F32),
        compiler_params=_cparams(("parallel", "arbitrary")),
        name="glu_proj",
    )(y, w, w, x)


def _gate_weight_kernel(a_ref, b_ref, o_ref):
    o_ref[...] = jnp.dot(a_ref[:, 0:b_ref.shape[0]], b_ref[...],
                         precision=lax.Precision.HIGHEST, preferred_element_type=F32)


def _gate_weight(w_in, layer, w_gate):
    d = w_in.shape[1]
    return pl.pallas_call(
        _gate_weight_kernel,
        grid=(1,),
        in_specs=[pl.BlockSpec((d, LANES), lambda i: (layer, OFF_LR // LANES)),
                  pl.BlockSpec(w_gate.shape, lambda i: (0, 0))],
        out_specs=pl.BlockSpec((d, w_gate.shape[1]), lambda i: (0, 0)),
        out_shape=jax.ShapeDtypeStruct((d, w_gate.shape[1]), F32),
        name="gate_weight",
    )(w_in.reshape(-1, w_in.shape[-1]), w_gate)


def _store_paired(dst_ref, v, bsz, ts):
    for c in range(v.shape[1] // LANES):
        for b in range(bsz):
            dst_ref.at[c // 2][pl.ds((c % 2) * bsz + b, ts, stride=SUBLANES), :] = (
                v[b * ts:(b + 1) * ts, c * LANES:(c + 1) * LANES])


def _load_paired(src_ref, bsz, ts, nchunks):
    cols = []
    for c in range(nchunks):
        cols.append(jnp.concatenate(
            [src_ref.at[c // 2][pl.ds((c % 2) * bsz + b, ts, stride=SUBLANES), :]
             for b in range(bsz)], axis=0))
    return jnp.concatenate(cols, axis=1)


def _step_rows(t0, k):
    base = pl.multiple_of(t0 * (SCAN_UNROLL * SUBLANES), SCAN_UNROLL * SUBLANES)
    return pl.ds(base + k * SUBLANES, SUBLANES)


def _pair_rows(v, batch):
    out = []
    for j in range(v.shape[1] // (2 * LANES)):
        lo = jnp.broadcast_to(v[:, (2 * j) * LANES:(2 * j + 1) * LANES], (batch, LANES))
        hi = jnp.broadcast_to(v[:, (2 * j + 1) * LANES:(2 * j + 2) * LANES], (batch, LANES))
        out.append(jnp.concatenate([lo, hi], axis=0))
    return out


def _lru_kernel(zx_ref, zg_ref, cw_ref, cb_ref, wa_ref, ba_ref, wx_ref, bx_ref, lam_ref,
                y_ref, xe_ref, a_ref, u_ref, h_ref):
    bsz, ts, c = zx_ref.shape
    rows = bsz * ts
    npair = c // (2 * LANES)
    nblk = wa_ref.shape[0]
    wblk = wa_ref.shape[1]

    @pl.when(pl.program_id(0) == 0)
    def _():
        xe_ref[:, 0:SUBLANES, :] = jnp.zeros((bsz, SUBLANES, c), F32)
        h_ref[...] = jnp.zeros(h_ref.shape, F32)

    x = zx_ref[...].astype(F32)
    xe_ref[:, SUBLANES:SUBLANES + ts, :] = x
    xc = cb_ref[...][None]
    for k in range(CONV_WIDTH):
        off = SUBLANES - (CONV_WIDTH - 1) + k
        xc = xc + cw_ref[k:k + 1, :][None] * xe_ref[:, off:off + ts, :]
    xe_ref[:, 0:SUBLANES, :] = x[:, ts - SUBLANES:, :]
    xc = xc.reshape(rows, c)

    xcb = xc.astype(BF16)
    ra, ia = [], []
    for b in range(nblk):
        xs = xcb[:, b * wblk:(b + 1) * wblk]
        ra.append(jnp.dot(xs, wa_ref[b], preferred_element_type=F32))
        ia.append(jnp.dot(xs, wx_ref[b], preferred_element_type=F32))
    r = jax.nn.sigmoid(jnp.concatenate(ra, axis=1) + ba_ref[...])
    i = jax.nn.sigmoid(jnp.concatenate(ia, axis=1) + bx_ref[...])
    log_a = (-LRU_C) * r * _softplus(-lam_ref[...])
    a = jnp.exp(log_a)
    mult = jnp.sqrt(1.0 - a * a)
    u = mult * (i * xc)
    _store_paired(a_ref, a, bsz, ts)
    _store_paired(u_ref, u, bsz, ts)

    def step(t0, hs):
        hs = list(hs)
        for k in range(SCAN_UNROLL):
            rs = _step_rows(t0, k)
            for j in range(npair):
                hs[j] = a_ref[j, rs, :] * hs[j] + u_ref[j, rs, :]
                u_ref[j, rs, :] = hs[j]
        return tuple(hs)

    hs = lax.fori_loop(0, ts // SCAN_UNROLL, step, tuple(h_ref[j] for j in range(npair)))
    for j in range(npair):
        h_ref[j] = hs[j]

    h = _load_paired(u_ref, bsz, ts, c // LANES)
    g = zg_ref[...].astype(F32).reshape(rows, c)
    y_ref[...] = (h * _gelu_tanh(g)).astype(y_ref.dtype).reshape(bsz, ts, c)


def _lru(z3, conv_w, conv_b, wa_bd, ba, wx_bd, bx, lam):
    bsz, s, _ = z3.shape
    assert 2 * bsz == SUBLANES
    ts = min(TS_LRU, s)
    c = D_LRU
    rows = bsz * ts
    full = lambda shape: pl.BlockSpec(shape, lambda i: (0,) * len(shape))
    return pl.pallas_call(
        _lru_kernel,
        grid=(s // ts,),
        in_specs=[pl.BlockSpec((bsz, ts, c), lambda i: (0, i, OFF_LRU_X // c)),
                  pl.BlockSpec((bsz, ts, c), lambda i: (0, i, OFF_LRU_G // c)),
                  full(conv_w.shape), full(conv_b.shape), full(wa_bd.shape), full(ba.shape),
                  full(wx_bd.shape), full(bx.shape), full(lam.shape)],
        out_specs=pl.BlockSpec((bsz, ts, c), lambda i: (0, i, 0)),
        out_shape=jax.ShapeDtypeStruct((bsz, s, c), BF16),
        scratch_shapes=[pltpu.VMEM((bsz, ts + SUBLANES, c), F32),
                        pltpu.VMEM((c // (2 * LANES), ts * SUBLANES, LANES), F32),
                        pltpu.VMEM((c // (2 * LANES), ts * SUBLANES, LANES), F32),
                        pltpu.VMEM((c // (2 * LANES), SUBLANES, LANES), F32)],
        compiler_params=_cparams(("arbitrary",)),
        name="rg_lru",
    )(z3, z3, conv_w, conv_b, wa_bd, ba, wx_bd, bx, lam)


def _gla_kernel(q_ref, k_ref, v_ref, og_ref, gl_ref, bg_ref, hn_ref, y_ref, st_ref):
    tc = q_ref.shape[1]
    cs = GLA_CHUNK

    @pl.when(pl.program_id(1) == 0)
    def _():
        st_ref[...] = jnp.zeros(st_ref.shape, F32)

    row = lax.broadcasted_iota(jnp.int32, (tc, tc), 0)
    col = lax.broadcasted_iota(jnp.int32, (tc, tc), 1)
    blk = jnp.where(row // cs == col // cs, 1.0, 0.0)
    tri = jnp.where(row >= col, blk, 0.0)
    causal = tri > 0.0
    for h in range(GLA_HEADS):
        dk = slice(h * GLA_DK, (h + 1) * GLA_DK)
        dv = slice(h * GLA_DV, (h + 1) * GLA_DV)
        x = gl_ref[0, :, dk].astype(F32) + bg_ref[:, dk]
        gk = (jnp.minimum(x, 0.0) - jnp.log1p(jnp.exp(-jnp.abs(x)))) * (1.0 / GLA_TAU)
        bcum = jnp.dot(tri, gk, precision=lax.Precision.HIGHEST, preferred_element_type=F32)
        btot = jnp.dot(blk, gk, precision=lax.Precision.HIGHEST, preferred_element_type=F32)
        q = q_ref[0, :, dk].astype(F32) * (GLA_DK ** -0.5)
        k = k_ref[0, :, dk].astype(F32)
        v = v_ref[0, :, dv]
        qs = (q * jnp.exp(bcum)).astype(BF16)
        ks = (k * jnp.exp(-bcum)).astype(BF16)
        ke = (k * jnp.exp(btot - bcum)).astype(BF16)
        att = lax.dot_general(qs, ks, (((1,), (1,)), ((), ())), preferred_element_type=F32)
        att = jnp.where(causal, att, 0.0).astype(BF16)
        o_intra = jnp.dot(att, v, preferred_element_type=F32)
        st = st_ref[h]
        outs = []
        for c in range(tc // cs):
            sl = slice(c * cs, (c + 1) * cs)
            outs.append(o_intra[sl] + lax.dot_general(
                qs[sl], st.astype(BF16), (((1,), (1,)), ((), ())), preferred_element_type=F32))
            ds = lax.dot_general(v[sl], ke[sl], (((0,), (0,)), ((), ())),
                                 preferred_element_type=F32)
            st = jnp.exp(btot[c * cs:c * cs + 1, :]) * st + ds
        st_ref[h] = st
        o = jnp.concatenate(outs, axis=0)
        o = o * lax.rsqrt(jnp.mean(o * o, axis=-1, keepdims=True) + EPS)
        y = (o * hn_ref[:, dv]) * _silu(og_ref[0, :, dv].astype(F32))
        y_ref[0, :, dv] = y.astype(y_ref.dtype)


def _gla(z3, b_gate, head_norm):
    bsz, s, _ = z3.shape
    tc = min(TC_GLA, s)
    qb, kb, vb, ob, gb = (OFF_Q // GLA_QK, OFF_K // GLA_QK, OFF_V // GLA_V, OFF_OG // GLA_V,
                          OFF_LR // GLA_QK)
    return pl.pallas_call(
        _gla_kernel,
        grid=(bsz, s // tc),
        in_specs=[pl.BlockSpec((1, tc, GLA_QK), lambda b, i: (b, i, qb)),
                  pl.BlockSpec((1, tc, GLA_QK), lambda b, i: (b, i, kb)),
                  pl.BlockSpec((1, tc, GLA_V), lambda b, i: (b, i, vb)),
                  pl.BlockSpec((1, tc, GLA_V), lambda b, i: (b, i, ob)),
                  pl.BlockSpec((1, tc, GLA_QK), lambda b, i: (b, i, gb)),
                  pl.BlockSpec((1, GLA_QK), lambda b, i: (0, 0)),
                  pl.BlockSpec((1, GLA_V), lambda b, i: (0, 0))],
        out_specs=pl.BlockSpec((1, tc, GLA_V), lambda b, i: (b, i, 0)),
        out_shape=jax.ShapeDtypeStruct((bsz, s, GLA_V), BF16),
        scratch_shapes=[pltpu.VMEM((GLA_HEADS, GLA_DV, GLA_DK), F32)],
        compiler_params=_cparams(("parallel", "arbitrary")),
        name="gla",
    )(z3, z3, z3, z3, z3, b_gate, head_norm)


def _s5_kernel(u_ref, lre_ref, lim_ref, lst_ref, bre_ref, bim_ref, cre_ref, cim_ref, d_ref,
               y_ref, wbr_ref, wbi_ref, ar_ref, ai_ref, sr_ref, si_ref, hr_ref, hi_ref):
    bsz, ts, cb = u_ref.shape
    rows = bsz * ts
    nst = lre_ref.shape[2]
    npair = nst // (2 * LANES)

    @pl.when(pl.program_id(1) == 0)
    def _():
        lr = jnp.minimum(lre_ref[0], S5_MAX_RE)
        li = lim_ref[0]
        dt = jnp.exp(lst_ref[0])
        mag = jnp.exp(lr * dt)
        ab_re = mag * jnp.cos(li * dt)
        ab_im = mag * jnp.sin(li * dt)
        den = lr * lr + li * li
        coef_re = ((ab_re - 1.0) * lr + ab_im * li) / den
        coef_im = (ab_im * lr - (ab_re - 1.0) * li) / den
        bre = bre_ref[0]
        bim = bim_ref[0]
        wbr_ref[...] = (coef_re * bre - coef_im * bim).astype(BF16)
        wbi_ref[...] = (coef_re * bim + coef_im * bre).astype(BF16)
        for j, v in enumerate(_pair_rows(ab_re, bsz)):
            ar_ref[j] = v
        for j, v in enumerate(_pair_rows(ab_im, bsz)):
            ai_ref[j] = v
        hr_ref[...] = jnp.zeros(hr_ref.shape, F32)
        hi_ref[...] = jnp.zeros(hi_ref.shape, F32)

    u = u_ref[...].reshape(rows, cb)
    _store_paired(sr_ref, jnp.dot(u, wbr_ref[...], preferred_element_type=F32), bsz, ts)
    _store_paired(si_ref, jnp.dot(u, wbi_ref[...], preferred_element_type=F32), bsz, ts)

    ars = [ar_ref[j] for j in range(npair)]
    ais = [ai_ref[j] for j in range(npair)]

    def step(t0, carry):
        hr, hi = list(carry[0]), list(carry[1])
        for k in range(SCAN_UNROLL):
            rs = _step_rows(t0, k)
            for j in range(npair):
                nr = ars[j] * hr[j] - ais[j] * hi[j] + sr_ref[j, rs, :]
                ni = ars[j] * hi[j] + ais[j] * hr[j] + si_ref[j, rs, :]
                hr[j], hi[j] = nr, ni
                sr_ref[j, rs, :] = nr
                si_ref[j, rs, :] = ni
        return tuple(hr), tuple(hi)

    init = (tuple(hr_ref[j] for j in range(npair)), tuple(hi_ref[j] for j in range(npair)))
    hr, hi = lax.fori_loop(0, ts // SCAN_UNROLL, step, init)
    for j in range(npair):
        hr_ref[j] = hr[j]
        hi_ref[j] = hi[j]

    h_re = _load_paired(sr_ref, bsz, ts, nst // LANES).astype(BF16)
    h_im = _load_paired(si_ref, bsz, ts, nst // LANES).astype(BF16)
    y = jnp.dot(h_re, cre_ref[0], preferred_element_type=F32)
    y = y - jnp.dot(h_im, cim_ref[0], preferred_element_type=F32)
    y = y + d_ref[...] * u.astype(F32)
    y_ref[...] = _gelu_tanh(y).astype(y_ref.dtype).reshape(bsz, ts, cb)


def _s5(u3, lre, lim, lst, bre_bd, bim_bd, cre_bd, cim_bd, d_skip):
    bsz, s, c = u3.shape
    assert 2 * bsz == SUBLANES
    ts = min(TS_S5, s)
    cb = S5_CB
    ncb = c // cb
    nst = lre.shape[2]
    rows = bsz * ts
    npair = nst // (2 * LANES)
    blk = lambda shape: pl.BlockSpec((1,) + shape, lambda ci, i: (ci, 0, 0))
    return pl.pallas_call(
        _s5_kernel,
        grid=(ncb, s // ts),
        in_specs=[pl.BlockSpec((bsz, ts, cb), lambda ci, i: (0, i, ci)),
                  blk((1, nst)), blk((1, nst)), blk((1, nst)),
                  blk((cb, nst)), blk((cb, nst)), blk((nst, cb)), blk((nst, cb)),
                  pl.BlockSpec((1, cb), lambda ci, i: (0, ci))],
        out_specs=pl.BlockSpec((bsz, ts, cb), lambda ci, i: (0, i, ci)),
        out_shape=jax.ShapeDtypeStruct((bsz, s, c), BF16),
        scratch_shapes=[pltpu.VMEM((cb, nst), BF16), pltpu.VMEM((cb, nst), BF16),
                        pltpu.VMEM((npair, SUBLANES, LANES), F32),
                        pltpu.VMEM((npair, SUBLANES, LANES), F32),
                        pltpu.VMEM((npair, ts * SUBLANES, LANES), F32),
                        pltpu.VMEM((npair, ts * SUBLANES, LANES), F32),
                        pltpu.VMEM((npair, SUBLANES, LANES), F32),
                        pltpu.VMEM((npair, SUBLANES, LANES), F32)],
        compiler_params=_cparams(("parallel", "arbitrary")),
        name="s5",
    )(u3, lre, lim, lst, bre_bd, bim_bd, cre_bd, cim_bd, d_skip)


ROUTE_GROUP_LANE0 = 0
ROUTE_EXPERT_LANE0 = N_EXPERT_GROUPS
RI_E0, RI_E1, RI_C0, RI_C1, RI_R0, RI_R1 = 0, 1, 2, 3, 4, 5


def _router_kernel(x_ref, g_ref, w_ref, b_ref, ri_ref, rt_ref, cnt_ref, run_ref, wh_ref, wl_ref):
    tm = x_ref.shape[0]
    neg = -jnp.inf

    @pl.when(pl.program_id(0) == 0)
    def _():
        run_ref[...] = jnp.zeros(run_ref.shape, F32)
        w = w_ref[...]
        wh = w.astype(BF16)
        wh_ref[...] = wh
        wl_ref[...] = (w - wh.astype(F32)).astype(BF16)

    xn = _rms(x_ref[...], g_ref[...])
    xh = xn.astype(BF16)
    xl = (xn - xh.astype(F32)).astype(BF16)
    logits = (jnp.dot(xh, wh_ref[...], preferred_element_type=F32)
              + jnp.dot(xl, wh_ref[...], preferred_element_type=F32)
              + jnp.dot(xh, wl_ref[...], preferred_element_type=F32)) + b_ref[...]
    lane = lax.broadcasted_iota(jnp.int32, (tm, LANES), 1)

    def first_lane(mask):
        return jnp.min(jnp.where(mask, lane, LANES), axis=1, keepdims=True)

    lg = jnp.where(lane < N_EXPERT_GROUPS, logits, neg)
    mg = jnp.max(lg, axis=1, keepdims=True)
    sg = jnp.sum(jnp.exp(lg - mg), axis=1, keepdims=True)
    gate_g = 1.0 / sg
    g_idx = first_lane(lg == mg)
    lo = ROUTE_EXPERT_LANE0 + EXPERTS_PER_GROUP * g_idx
    in_group = jnp.abs(2 * (lane - lo) - (EXPERTS_PER_GROUP - 1)) < EXPERTS_PER_GROUP
    le = jnp.where(in_group, logits, neg)
    m1 = jnp.max(le, axis=1, keepdims=True)
    i1 = first_lane(le == m1)
    le2 = jnp.where(lane == i1, neg, le)
    m2 = jnp.max(le2, axis=1, keepdims=True)
    i2 = first_lane(le2 == m2)
    se = jnp.sum(jnp.exp(le - m1), axis=1, keepdims=True)
    p1 = 1.0 / se
    p2 = jnp.exp(m2 - m1) / se
    c0 = gate_g * (p1 / (p1 + p2))
    c1 = gate_g * (p2 / (p1 + p2))

    sel0 = lane == i1
    sel1 = lane == i2
    onehot = jnp.where(sel0, 1.0, jnp.where(sel1, 1.0, 0.0))
    r_i = lax.broadcasted_iota(jnp.int32, (tm, tm), 0)
    c_i = lax.broadcasted_iota(jnp.int32, (tm, tm), 1)
    before = jnp.where(c_i < r_i, 1.0, 0.0).astype(BF16)
    prefix = jnp.dot(before, onehot.astype(BF16), preferred_element_type=F32) + run_ref[...]
    rank0 = jnp.sum(jnp.where(sel0, prefix, 0.0), axis=1, keepdims=True)
    rank1 = jnp.sum(jnp.where(sel1, prefix, 0.0), axis=1, keepdims=True)
    run_ref[...] = run_ref[...] + jnp.sum(onehot, axis=0, keepdims=True)
    cnt_ref[...] = run_ref[...]

    e0 = (i1 - ROUTE_EXPERT_LANE0).astype(F32)
    e1 = (i2 - ROUTE_EXPERT_LANE0).astype(F32)
    rec = jnp.zeros((tm, LANES), F32)
    for ln, val in ((RI_E0, e0), (RI_E1, e1), (RI_C0, c0), (RI_C1, c1), (RI_R0, rank0),
                    (RI_R1, rank1)):
        rec = jnp.where(lane == ln, val, rec)
    ri_ref[...] = rec
    rt_ref[...] = rec.T[0:SUBLANES, :]


def _router(x, g, w_route, b_route):
    t, d = x.shape
    tm = min(TM_ROUTE, t)
    return pl.pallas_call(
        _router_kernel,
        grid=(t // tm,),
        in_specs=[pl.BlockSpec((tm, d), lambda i: (i, 0)),
                  pl.BlockSpec((1, d), lambda i: (0, 0)),
                  pl.BlockSpec((d, LANES), lambda i: (0, 0)),
                  pl.BlockSpec((1, LANES), lambda i: (0, 0))],
        out_specs=[pl.BlockSpec((tm, LANES), lambda i: (i, 0)),
                   pl.BlockSpec((SUBLANES, tm), lambda i: (0, i)),
                   pl.BlockSpec((1, LANES), lambda i: (0, 0))],
        out_shape=[jax.ShapeDtypeStruct((t, LANES), F32),
                   jax.ShapeDtypeStruct((SUBLANES, t), F32),
                   jax.ShapeDtypeStruct((1, LANES), F32)],
        scratch_shapes=[pltpu.VMEM((1, LANES), F32), pltpu.VMEM((d, LANES), BF16),
                        pltpu.VMEM((d, LANES), BF16)],
        compiler_params=_cparams(("arbitrary",)),
        name="router",
    )(x, g, w_route, b_route)


def _piece_groups(n_rows, weights):
    total = sum(weights)
    bounds = [round(n_rows * sum(weights[:k]) / total) for k in range(len(weights) + 1)]
    return [range(bounds[k], bounds[k + 1]) for k in range(len(weights))]


def _dispatch_kernel(n_rows, pos_ref, zrow_ref, x_ref, g_ref, xs_hbm, xbuf, zbuf, sem, zsem):
    i = pl.program_id(0)
    nb = pl.num_programs(0)
    tm, d = x_ref.shape
    t = nb * tm
    tme = zbuf.shape[0]
    slot = i % DISPATCH_BUFS

    def scatter_wait(s):
        for k in range(2):
            pltpu.make_async_copy(xbuf.at[s], xs_hbm.at[pl.ds(0, tm), :], sem.at[s]).wait()

    @pl.when(i == 0)
    def _():
        zbuf[...] = jnp.zeros(zbuf.shape, F32)

        def fill(e):
            row = pl.multiple_of(jnp.maximum(zrow_ref[e], 0), tme)
            return pltpu.make_async_copy(zbuf, xs_hbm.at[pl.ds(row, tme), :], zsem)

        for e in range(n_rows):
            @pl.when(zrow_ref[e] >= 0)
            def _():
                fill(e).start()

        for e in range(n_rows):
            @pl.when(zrow_ref[e] >= 0)
            def _():
                fill(e).wait()

    @pl.when(i >= DISPATCH_BUFS)
    def _():
        scatter_wait(slot)

    xbuf[slot] = _rms(x_ref[...], g_ref[...])
    for r in range(tm):
        for k in range(2):
            dst = pos_ref[k * t + i * tm + r]
            pltpu.make_async_copy(xbuf.at[slot, pl.ds(r, 1), :], xs_hbm.at[pl.ds(dst, 1), :],
                                  sem.at[slot]).start(priority=k)

    @pl.when(i == nb - 1)
    def _():
        for a in range(min(DISPATCH_BUFS, nb)):
            scatter_wait((i - a) % DISPATCH_BUFS)


def _dispatch(x, g, pos_flat, zero_rows, n_slots, tme):
    t, d = x.shape
    tm = min(TM_DISPATCH, t)
    n_e = zero_rows.shape[0]
    return pl.pallas_call(
        functools.partial(_dispatch_kernel, n_e),
        grid_spec=pltpu.PrefetchScalarGridSpec(
            num_scalar_prefetch=2,
            grid=(t // tm,),
            in_specs=[pl.BlockSpec((tm, d), lambda i, *_: (i, 0)),
                      pl.BlockSpec((1, d), lambda i, *_: (0, 0))],
            out_specs=pl.BlockSpec(memory_space=pl.ANY),
            scratch_shapes=[pltpu.VMEM((DISPATCH_BUFS, tm, d), F32),
                            pltpu.VMEM((tme, d), F32),
                            pltpu.SemaphoreType.DMA((DISPATCH_BUFS,)),
                            pltpu.SemaphoreType.DMA(())]),
        out_shape=jax.ShapeDtypeStruct((n_slots, d), F32),
        compiler_params=_cparams(("arbitrary",)),
        name="dispatch",
    )(pos_flat, zero_rows, x, g)


def _experts_kernel(layer, eidx_ref, eseq_ref, meta_ref, x_ref, w1_hbm, w3_hbm, w2_hbm, y_ref,
                    ws1, ws3, ws2, w1b_ref, w3b_ref, w2b_ref, wsem):
    i = pl.program_id(0)
    nv = meta_ref[0]
    ne = meta_ref[1]

    def weight_copies(k):
        e = eseq_ref[k]
        s = k % 2
        return [pltpu.make_async_copy(w_hbm.at[layer, e], ws.at[s], wsem.at[s])
                for w_hbm, ws in ((w1_hbm, ws1), (w3_hbm, ws3), (w2_hbm, ws2))]

    @pl.when(i == 0)
    def _():
        for c in weight_copies(0):
            c.start(priority=WEIGHT_DMA_PRIORITY)

        @pl.when(ne > 1)
        def _():
            for c in weight_copies(1):
                c.start(priority=WEIGHT_DMA_PRIORITY)

    @pl.when(i >= nv)
    def _():
        y_ref[...] = jnp.zeros(y_ref.shape, y_ref.dtype)

    @pl.when(i < nv)
    def _():
        k = eidx_ref[i]

        @pl.when(jnp.logical_or(i == 0, k != eidx_ref[jnp.maximum(i - 1, 0)]))
        def _():
            for c in weight_copies(k):
                c.wait()
            s = k % 2
            w1b_ref[...] = ws1[s].astype(BF16)
            w3b_ref[...] = ws3[s].astype(BF16)
            w2b_ref[...] = ws2[s].astype(BF16)

            @pl.when(k + 2 < ne)
            def _():
                for c in weight_copies(k + 2):
                    c.start(priority=WEIGHT_DMA_PRIORITY)

        xb = x_ref[...].astype(BF16)
        h = _silu(jnp.dot(xb, w1b_ref[...], preferred_element_type=F32))
        h = h * jnp.dot(xb, w3b_ref[...], preferred_element_type=F32)
        y_ref[...] = jnp.dot(h.astype(BF16), w2b_ref[...], preferred_element_type=F32)


def _experts(tile_eidx, expert_seq, meta, xs, w1, w3, w2, layer):
    d, f = w1.shape[-2:]
    nt = tile_eidx.shape[0]
    tme = xs.shape[0] // nt
    hbm = pl.BlockSpec(memory_space=pl.ANY)
    return pl.pallas_call(
        functools.partial(_experts_kernel, layer),
        grid_spec=pltpu.PrefetchScalarGridSpec(
            num_scalar_prefetch=3,
            grid=(nt,),
            in_specs=[pl.BlockSpec((tme, d), lambda i, eidx, eseq, m: (jnp.minimum(i, m[0] - 1), 0)),
                      hbm, hbm, hbm],
            out_specs=pl.BlockSpec((tme, d), lambda i, *_: (i, 0)),
            scratch_shapes=[pltpu.VMEM((2, d, f), F32), pltpu.VMEM((2, d, f), F32),
                            pltpu.VMEM((2, f, d), F32),
                            pltpu.VMEM((d, f), BF16), pltpu.VMEM((d, f), BF16),
                            pltpu.VMEM((f, d), BF16),
                            pltpu.SemaphoreType.DMA((2,))]),
        out_shape=jax.ShapeDtypeStruct((nt * tme, d), F32),
        compiler_params=_cparams(("arbitrary",)),
        name="experts",
    )(tile_eidx, expert_seq, meta, xs, w1, w3, w2)


def _route_plan(rt, counts, tme):
    t = rt.shape[1]
    nt = 2 * t // tme + N_EXPERTS
    e = rt[RI_E0:RI_E1 + 1].astype(jnp.int32)
    rank = rt[RI_R0:RI_R1 + 1].astype(jnp.int32)
    cnt = counts[0, ROUTE_EXPERT_LANE0:ROUTE_EXPERT_LANE0 + N_EXPERTS].astype(jnp.int32)
    padded = ((cnt + tme - 1) // tme) * tme
    ends = jnp.cumsum(padded)
    offs = ends - padded
    experts = jnp.arange(N_EXPERTS, dtype=jnp.int32)[:, None, None]
    pos = jnp.sum(jnp.where(e[None] == experts, offs[:, None, None], 0), axis=0) + rank
    n_valid = (ends[-1] // tme).astype(jnp.int32)
    starts = jnp.arange(nt, dtype=jnp.int32) * tme
    tile_expert = jnp.sum((starts[:, None] >= ends[None, :]).astype(jnp.int32), axis=1)
    last = jnp.take(tile_expert, jnp.maximum(n_valid - 1, 0))
    tile_expert = jnp.where(jnp.arange(nt) < n_valid, tile_expert, last)
    tile_expert = jnp.minimum(tile_expert, N_EXPERTS - 1).astype(jnp.int32)
    used_cum = jnp.cumsum((cnt > 0).astype(jnp.int32))
    n_used = used_cum[-1]
    ordinals = jnp.arange(N_EXPERTS, dtype=jnp.int32)
    expert_seq = jnp.sum((used_cum[None, :] <= ordinals[:, None]).astype(jnp.int32), axis=1)
    expert_seq = jnp.minimum(expert_seq, N_EXPERTS - 1).astype(jnp.int32)
    tile_eidx = (jnp.take(used_cum, tile_expert) - 1).astype(jnp.int32)
    meta = jnp.stack([n_valid, n_used]).astype(jnp.int32)
    pos_flat = pos.reshape(-1)
    tail = jnp.arange(2 * t // tme, nt, dtype=jnp.int32)
    zero_rows = jnp.concatenate([jnp.where(cnt > 0, ends - tme, -1),
                                 jnp.where(tail >= n_valid, tail * tme, -1)]).astype(jnp.int32)
    return tile_eidx, expert_seq, meta, zero_rows, pos_flat, nt


def _ple_kernel(final, pos_ref, ys_hbm, x_ref, ri_ref, g_ref, wg_ref, bg_ref, p_ref, wp_ref,
                gf_ref, o_ref, ybuf, sem):
    i = pl.program_id(0)
    nb = pl.num_programs(0)
    tm, d = o_ref.shape
    t = nb * tm
    slot = i % PLE_ROW_BUFS

    def gather_row(blk, r, s):
        for k in range(2):
            src = pos_ref[k * t + blk * tm + r]
            pltpu.make_async_copy(ys_hbm.at[pl.ds(src, 1), :], ybuf.at[s, k, pl.ds(r, 1), :],
                                  sem.at[s]).start(priority=k)

    def gather_block(blk, s):
        def body(r0, c):
            for k in range(DMA_UNROLL):
                gather_row(blk, r0 * DMA_UNROLL + k, s)
            return c

        lax.fori_loop(0, tm // DMA_UNROLL, body, 0)

    def gather_wait(s):
        for k in range(2):
            pltpu.make_async_copy(ys_hbm.at[pl.ds(0, tm), :], ybuf.at[s, k], sem.at[s]).wait()

    @pl.when(i == 0)
    def _():
        gather_block(0, 0)
        gather_block(jnp.minimum(1, nb - 1), 1)

    gather_wait(slot)
    nxt = jnp.minimum(i + 2, nb - 1)
    other = (i + 2) % PLE_ROW_BUFS
    groups = _piece_groups(tm, [1] * PLE_PIECES)
    ri = ri_ref[...]
    x2 = x_ref[...] + ri[:, RI_C0:RI_C0 + 1] * ybuf[slot, 0] + ri[:, RI_C1:RI_C1 + 1] * ybuf[slot, 1]
    o_ref[...] = x2
    xn = _rms(x2, g_ref[...]).astype(BF16)
    pb = p_ref[0].astype(BF16)
    w = d // PLE_PIECES
    for c in range(PLE_PIECES):
        cs = slice(c * w, (c + 1) * w)
        gate = jax.nn.sigmoid(jnp.dot(xn, wg_ref[0, :, cs], preferred_element_type=F32)
                              + bg_ref[:, cs])
        proj = jnp.dot(pb, wp_ref[0, :, cs], preferred_element_type=F32)
        o_ref[:, cs] = o_ref[:, cs] + gate * proj
        for r in groups[c]:
            gather_row(nxt, r, other)
    if final:
        o_ref[...] = _rms(o_ref[...], gf_ref[...])

    @pl.when(i == nb - 1)
    def _():
        gather_wait((i + 1) % PLE_ROW_BUFS)
        gather_wait(other)


def _ple(x, ys, pos_flat, rinfo, g, wg, bg, p, wp, layer, g_final):
    t, d = x.shape
    tm = min(TM_PLE, t)
    dp = p.shape[-1]
    final = layer == p.shape[0] - 1
    return pl.pallas_call(
        functools.partial(_ple_kernel, final),
        grid_spec=pltpu.PrefetchScalarGridSpec(
            num_scalar_prefetch=1,
            grid=(t // tm,),
            in_specs=[pl.BlockSpec(memory_space=pl.ANY),
                      pl.BlockSpec((tm, d), lambda i, pos: (i, 0)),
                      pl.BlockSpec((tm, LANES), lambda i, pos: (i, 0)),
                      pl.BlockSpec((1, d), lambda i, pos: (0, 0)),
                      pl.BlockSpec((1, d, d), lambda i, pos: (layer, 0, 0)),
                      pl.BlockSpec((1, d), lambda i, pos: (0, 0)),
                      pl.BlockSpec((1, tm, dp), lambda i, pos: (layer, i, 0)),
                      pl.BlockSpec((1, dp, d), lambda i, pos: (layer, 0, 0)),
                      pl.BlockSpec((1, d), lambda i, pos: (0, 0))],
            out_specs=pl.BlockSpec((tm, d), lambda i, pos: (i, 0)),
            scratch_shapes=[pltpu.VMEM((PLE_ROW_BUFS, 2, tm, d), F32),
                            pltpu.SemaphoreType.DMA((PLE_ROW_BUFS,))]),
        out_shape=jax.ShapeDtypeStruct((t, d), F32),
        compiler_params=_cparams(("arbitrary",)),
        name="ple",
    )(pos_flat, ys, x, rinfo, g, wg, bg, p, wp, g_final)


def _moe_ple(x, layer, g_ffn, w_rg, b_rg, w_re, b_re, w1, w3, w2, g_ple, wg, bg, p, wp,
             g_final):
    d = x.shape[1]
    pad = LANES - N_EXPERT_GROUPS - N_EXPERTS
    w_route = jnp.concatenate([w_rg, w_re, jnp.zeros((d, pad), F32)], axis=1)
    b_route = jnp.concatenate([b_rg, b_re, jnp.zeros((pad,), F32)])[None]
    rinfo, rt, counts = _router(x, g_ffn[None], w_route, b_route)
    tile_eidx, expert_seq, meta, zero_rows, pos_flat, nt = _route_plan(rt, counts, TM_EXP)
    xs = _dispatch(x, g_ffn[None], pos_flat, zero_rows, nt * TM_EXP, TM_EXP)
    ys = _experts(tile_eidx, expert_seq, meta, xs, w1, w3, w2, layer)
    return _ple(x, ys, pos_flat, rinfo, g_ple[None], wg, bg[None], p, wp, layer, g_final[None])


def _block_diag(w, nb):
    h, d, _ = w.shape
    w4 = w.reshape(h // nb, nb, d, d)
    out = jnp.einsum('cadk,ab->cadbk', w4, jnp.eye(nb, dtype=w.dtype))
    return out.reshape(h // nb, nb * d, nb * d)


def _even_layer(x, bsz, e, g, w_in, conv_w, conv_b, w_a, b_a, w_x, b_x, lam, w_gate, b_gate,
                head_norm, w_out):
    t = x.shape[0]
    w_gk = _gate_weight(w_in, e, w_gate)
    z = _norm_matmul(x, g[None], w_in, e, OFF_LR, w_gk)
    z3 = z.reshape(bsz, t // bsz, D_EVEN_Z)
    heads_per_blk = 2 * LANES // LRU_HEAD_DIM
    y_lru = _lru(z3, conv_w, conv_b[None], _block_diag(w_a, heads_per_blk).astype(BF16),
                 b_a[None], _block_diag(w_x, heads_per_blk).astype(BF16), b_x[None], lam[None])
    y_gla = _gla(z3, b_gate[None], head_norm[None])
    return _mm2_res(y_lru.reshape(t, D_LRU), y_gla.reshape(t, GLA_V), w_out, e, x)


def _odd_layer(x, bsz, o, g, w_in, lam_re, lam_im, log_step, b_re, b_im, c_re, c_im, d_skip,
               w_glu):
    t = x.shape[0]
    u = _norm_matmul(x, g[None], w_in, o, D_S5)
    gpb = S5_CB // S5_GROUP
    ncb = S5_GROUPS // gpb
    nst = gpb * S5_STATE
    eye = jnp.eye(gpb, dtype=F32)
    lay = lambda a: a.reshape(ncb, 1, nst)
    lst = jnp.repeat(log_step, S5_STATE)
    bexp = lambda b: jnp.einsum('cgph,gk->cghkp', b.reshape(ncb, gpb, S5_STATE, S5_GROUP),
                                eye).reshape(ncb, S5_CB, nst)
    cexp = lambda c: jnp.einsum('cgop,gk->cgpko', c.reshape(ncb, gpb, S5_GROUP, S5_STATE),
                                eye).reshape(ncb, nst, S5_CB).astype(BF16)
    y = _s5(u.reshape(bsz, t // bsz, D_S5), lay(lam_re), lay(lam_im), lay(lst),
            bexp(b_re), bexp(b_im), cexp(c_re), cexp(c_im), d_skip[None])
    return _glu_res(y.reshape(t, D_S5), w_glu, o, x)


def kernel(x, p, norm_mix, norm_ffn, norm_ple, norm_final, ev_w_in, lru_conv_w, lru_conv_b, lru_w_a, lru_b_a, lru_w_x, lru_b_x, lru_lambda, gla_w_gate, gla_b_gate, gla_norm, ev_w_out, od_w_in, s5_lambda_re, s5_lambda_im, s5_log_step, s5_b_re, s5_b_im, s5_c_re, s5_c_im, s5_d, od_w_glu, moe_w_router_group, moe_b_router_group, moe_w_router_expert, moe_b_router_expert, moe_w1, moe_w3, moe_w2, ple_w_gate, ple_b_gate, ple_w_proj):
    bsz, s, d = x.shape
    t = bsz * s
    depth = p.shape[0]
    h = x.reshape(t, d)
    p3 = p.reshape(depth, t, p.shape[-1])
    wg_bf = ple_w_gate.astype(BF16)
    wp_bf = ple_w_proj.astype(BF16)
    for l in range(depth):
        if l % 2 == 0:
            e = l // 2
            h = _even_layer(h, bsz, e, norm_mix[l], ev_w_in, lru_conv_w[e], lru_conv_b[e],
                            lru_w_a[e], lru_b_a[e], lru_w_x[e], lru_b_x[e], lru_lambda[e],
                            gla_w_gate[e], gla_b_gate[e], gla_norm[e], ev_w_out)
        else:
            o = l // 2
            h = _odd_layer(h, bsz, o, norm_mix[l], od_w_in, s5_lambda_re[o], s5_lambda_im[o],
                           s5_log_step[o], s5_b_re[o], s5_b_im[o], s5_c_re[o], s5_c_im[o],
                           s5_d[o], od_w_glu)
        h = _moe_ple(h, l, norm_ffn[l], moe_w_router_group[l], moe_b_router_group[l],
                     moe_w_router_expert[l], moe_b_router_expert[l], moe_w1, moe_w3,
                     moe_w2, norm_ple[l], wg_bf, ple_b_gate[l], p3, wp_bf, norm_final)
    return h.reshape(bsz, s, d)
```

```python
import functools
import math

import jax
import jax.numpy as jnp
from jax import lax
from jax.experimental import pallas as pl
from jax.experimental.pallas import tpu as pltpu

F32 = jnp.float32
BF16 = jnp.bfloat16

D_MODEL = 2048
D_LRU = 1024
LRU_HEADS = 16
LRU_HEAD_DIM = D_LRU // LRU_HEADS
CONV_WIDTH = 4
LRU_C = 8.0
GLA_HEADS = 4
GLA_DK = 128
GLA_DV = 256
GLA_QK = GLA_HEADS * GLA_DK
GLA_V = GLA_HEADS * GLA_DV
GLA_RANK = 16
GLA_TAU = 16.0
GLA_CHUNK = 64
OFF_LRU_X = 0
OFF_LRU_G = OFF_LRU_X + D_LRU
OFF_Q = OFF_LRU_G + D_LRU
OFF_K = OFF_Q + GLA_QK
OFF_V = OFF_K + GLA_QK
OFF_OG = OFF_V + GLA_V
OFF_LR = OFF_OG + GLA_V
D_EVEN_Z = OFF_LR + GLA_QK
D_S5 = 1024
S5_GROUP = 16
S5_GROUPS = D_S5 // S5_GROUP
S5_STATE = 64
S5_MAX_RE = -1e-4
N_EXPERT_GROUPS = 4
EXPERTS_PER_GROUP = 8
N_EXPERTS = N_EXPERT_GROUPS * EXPERTS_PER_GROUP
D_EXPERT = 512
D_PLE = 256
EPS = 1e-6

LANES = 128
SUBLANES = 8
VMEM_LIMIT = 56 * 1024 * 1024

TM_MM = 1024
TM_RES = 2048
TN_MM = 512
TS_LRU = 256
TC_GLA = 256
TS_S5 = 256
S5_CB = 256
TM_ROUTE = 512
TM_EXP = 256
TM_PLE = 256
PLE_PIECES = 4
PLE_ROW_BUFS = 3
TM_DISPATCH = 512
DISPATCH_BUFS = 4
WEIGHT_DMA_PRIORITY = 1
SCAN_UNROLL = 8
DMA_UNROLL = 8


def _cparams(sem):
    return pltpu.CompilerParams(dimension_semantics=sem, vmem_limit_bytes=VMEM_LIMIT)


def _rms(x, g):
    return x * lax.rsqrt(jnp.mean(x * x, axis=-1, keepdims=True) + EPS) * g


def _gelu_tanh(x):
    c = math.sqrt(2.0 / math.pi)
    return 0.5 * x * (1.0 + jnp.tanh(c * (x + 0.044715 * (x * x * x))))


def _softplus(x):
    return jnp.maximum(x, 0.0) + jnp.log1p(jnp.exp(-jnp.abs(x)))


def _silu(x):
    return x * jax.nn.sigmoid(x)


def _norm_matmul_kernel(nj, x_ref, g_ref, w_ref, *rest):
    o_ref, xn_ref = rest[-2:]
    j = pl.program_id(1)

    @pl.when(j == 0)
    def _():
        xn_ref[...] = _rms(x_ref[...], g_ref[...]).astype(BF16)

    @pl.when(j < nj)
    def _():
        o_ref[...] = jnp.dot(xn_ref[...], w_ref[...].astype(BF16),
                             preferred_element_type=F32).astype(o_ref.dtype)

    if len(rest) == 3:
        @pl.when(j >= nj)
        def _():
            o_ref[...] = jnp.dot(xn_ref[...], rest[0][...].astype(BF16),
                                 preferred_element_type=F32).astype(o_ref.dtype)


def _norm_matmul(x, g, w, layer, n_main, w_extra=None, tn=TN_MM):
    t, d = x.shape
    tm = min(TM_MM, t)
    nj = n_main // tn
    in_specs = [pl.BlockSpec((tm, d), lambda i, j: (i, 0)),
                pl.BlockSpec((1, d), lambda i, j: (0, 0)),
                pl.BlockSpec((d, tn), lambda i, j: (layer, jnp.minimum(j, nj - 1)))]
    args = [x, g, w.reshape(-1, w.shape[-1])]
    n_extra = 0
    if w_extra is not None:
        n_extra = 1
        in_specs.append(pl.BlockSpec((d, tn), lambda i, j: (0, 0)))
        args.append(w_extra)
    return pl.pallas_call(
        functools.partial(_norm_matmul_kernel, nj),
        grid=(t // tm, nj + n_extra),
        in_specs=in_specs,
        out_specs=pl.BlockSpec((tm, tn), lambda i, j: (i, j)),
        out_shape=jax.ShapeDtypeStruct((t, (nj + n_extra) * tn), BF16),
        scratch_shapes=[pltpu.VMEM((tm, d), BF16)],
        compiler_params=_cparams(("parallel", "arbitrary")),
        name="norm_matmul",
    )(*args)


def _mm2_res_kernel(a1_ref, a2_ref, w_ref, x_ref, o_ref):
    k1 = a1_ref.shape[1]
    acc = jnp.dot(a1_ref[...], w_ref[0, 0:k1, :].astype(BF16), preferred_element_type=F32)
    acc = acc + jnp.dot(a2_ref[...], w_ref[0, k1:, :].astype(BF16), preferred_element_type=F32)
    o_ref[...] = x_ref[...] + acc


def _mm2_res(a1, a2, w, layer, x):
    t, k1 = a1.shape
    k2 = a2.shape[1]
    n = w.shape[2]
    tm, tn = min(TM_RES, t), TN_MM
    return pl.pallas_call(
        _mm2_res_kernel,
        grid=(t // tm, n // tn),
        in_specs=[pl.BlockSpec((tm, k1), lambda i, j: (i, 0)),
                  pl.BlockSpec((tm, k2), lambda i, j: (i, 0)),
                  pl.BlockSpec((1, k1 + k2, tn), lambda i, j: (layer, 0, j)),
                  pl.BlockSpec((tm, tn), lambda i, j: (i, j))],
        out_specs=pl.BlockSpec((tm, tn), lambda i, j: (i, j)),
        out_shape=jax.ShapeDtypeStruct((t, n), F32),
        compiler_params=_cparams(("parallel", "arbitrary")),
        name="out_proj",
    )(a1, a2, w, x)


def _glu_res_kernel(y_ref, w1_ref, w2_ref, x_ref, o_ref):
    y = y_ref[...]
    z1 = jnp.dot(y, w1_ref[0].astype(BF16), preferred_element_type=F32)
    z2 = jnp.dot(y, w2_ref[0].astype(BF16), preferred_element_type=F32)
    o_ref[...] = x_ref[...] + z1 * jax.nn.sigmoid(z2)


def _glu_res(y, w, layer, x):
    t, k = y.shape
    n = w.shape[2] // 2
    tm, tn = min(TM_RES, t), TN_MM
    nj = n // tn
    return pl.pallas_call(
        _glu_res_kernel,
        grid=(t // tm, nj),
        in_specs=[pl.BlockSpec((tm, k), lambda i, j: (i, 0)),
                  pl.BlockSpec((1, k, tn), lambda i, j: (layer, 0, j)),
                  pl.BlockSpec((1, k, tn), lambda i, j: (layer, 0, j + nj)),
                  pl.BlockSpec((tm, tn), lambda i, j: (i, j))],
        out_specs=pl.BlockSpec((tm, tn), lambda i, j: (i, j)),
        out_shape=jax.ShapeDtypeStruct((t, n), F32),
        compiler_params=_cparams(("parallel", "arbitrary")),
        name="glu_proj",
    )(y, w, w, x)


def _gate_weight_kernel(a_ref, b_ref, o_ref):
    o_ref[...] = jnp.dot(a_ref[:, 0:b_ref.shape[0]], b_ref[...],
                         precision=lax.Precision.HIGHEST, preferred_element_type=F32)


def _gate_weight(w_in, layer, w_gate):
    d = w_in.shape[1]
    return pl.pallas_call(
        _gate_weight_kernel,
        grid=(1,),
        in_specs=[pl.BlockSpec((d, LANES), lambda i: (layer, OFF_LR // LANES)),
                  pl.BlockSpec(w_gate.shape, lambda i: (0, 0))],
        out_specs=pl.BlockSpec((d, w_gate.shape[1]), lambda i: (0, 0)),
        out_shape=jax.ShapeDtypeStruct((d, w_gate.shape[1]), F32),
        name="gate_weight",
    )(w_in.reshape(-1, w_in.shape[-1]), w_gate)


def _store_paired(dst_ref, v, bsz, ts):
    for c in range(v.shape[1] // LANES):
        for b in range(bsz):
            dst_ref.at[c // 2][pl.ds((c % 2) * bsz + b, ts, stride=SUBLANES), :] = (
                v[b * ts:(b + 1) * ts, c * LANES:(c + 1) * LANES])


def _load_paired(src_ref, bsz, ts, nchunks):
    cols = []
    for c in range(nchunks):
        cols.append(jnp.concatenate(
            [src_ref.at[c // 2][pl.ds((c % 2) * bsz + b, ts, stride=SUBLANES), :]
             for b in range(bsz)], axis=0))
    return jnp.concatenate(cols, axis=1)


def _step_rows(t0, k):
    base = pl.multiple_of(t0 * (SCAN_UNROLL * SUBLANES), SCAN_UNROLL * SUBLANES)
    return pl.ds(base + k * SUBLANES, SUBLANES)


def _pair_rows(v, batch):
    out = []
    for j in range(v.shape[1] // (2 * LANES)):
        lo = jnp.broadcast_to(v[:, (2 * j) * LANES:(2 * j + 1) * LANES], (batch, LANES))
        hi = jnp.broadcast_to(v[:, (2 * j + 1) * LANES:(2 * j + 2) * LANES], (batch, LANES))
        out.append(jnp.concatenate([lo, hi], axis=0))
    return out


def _lru_kernel(zx_ref, zg_ref, cw_ref, cb_ref, wa_ref, ba_ref, wx_ref, bx_ref, lam_ref,
                y_ref, xe_ref, a_ref, u_ref, h_ref):
    bsz, ts, c = zx_ref.shape
    rows = bsz * ts
    npair = c // (2 * LANES)
    nblk = wa_ref.shape[0]
    wblk = wa_ref.shape[1]

    @pl.when(pl.program_id(0) == 0)
    def _():
        xe_ref[:, 0:SUBLANES, :] = jnp.zeros((bsz, SUBLANES, c), F32)
        h_ref[...] = jnp.zeros(h_ref.shape, F32)

    x = zx_ref[...].astype(F32)
    xe_ref[:, SUBLANES:SUBLANES + ts, :] = x
    xc = cb_ref[...][None]
    for k in range(CONV_WIDTH):
        off = SUBLANES - (CONV_WIDTH - 1) + k
        xc = xc + cw_ref[k:k + 1, :][None] * xe_ref[:, off:off + ts, :]
    xe_ref[:, 0:SUBLANES, :] = x[:, ts - SUBLANES:, :]
    xc = xc.reshape(rows, c)

    xcb = xc.astype(BF16)
    ra, ia = [], []
    for b in range(nblk):
        xs = xcb[:, b * wblk:(b + 1) * wblk]
        ra.append(jnp.dot(xs, wa_ref[b], preferred_element_type=F32))
        ia.append(jnp.dot(xs, wx_ref[b], preferred_element_type=F32))
    r = jax.nn.sigmoid(jnp.concatenate(ra, axis=1) + ba_ref[...])
    i = jax.nn.sigmoid(jnp.concatenate(ia, axis=1) + bx_ref[...])
    log_a = (-LRU_C) * r * _softplus(-lam_ref[...])
    a = jnp.exp(log_a)
    mult = jnp.sqrt(1.0 - a * a)
    u = mult * (i * xc)
    _store_paired(a_ref, a, bsz, ts)
    _store_paired(u_ref, u, bsz, ts)

    def step(t0, hs):
        hs = list(hs)
        for k in range(SCAN_UNROLL):
            rs = _step_rows(t0, k)
            for j in range(npair):
                hs[j] = a_ref[j, rs, :] * hs[j] + u_ref[j, rs, :]
                u_ref[j, rs, :] = hs[j]
        return tuple(hs)

    hs = lax.fori_loop(0, ts // SCAN_UNROLL, step, tuple(h_ref[j] for j in range(npair)))
    for j in range(npair):
        h_ref[j] = hs[j]

    h = _load_paired(u_ref, bsz, ts, c // LANES)
    g = zg_ref[...].astype(F32).reshape(rows, c)
    y_ref[...] = (h * _gelu_tanh(g)).astype(y_ref.dtype).reshape(bsz, ts, c)


def _lru(z3, conv_w, conv_b, wa_bd, ba, wx_bd, bx, lam):
    bsz, s, _ = z3.shape
    assert 2 * bsz == SUBLANES
    ts = min(TS_LRU, s)
    c = D_LRU
    rows = bsz * ts
    full = lambda shape: pl.BlockSpec(shape, lambda i: (0,) * len(shape))
    return pl.pallas_call(
        _lru_kernel,
        grid=(s // ts,),
        in_specs=[pl.BlockSpec((bsz, ts, c), lambda i: (0, i, OFF_LRU_X // c)),
                  pl.BlockSpec((bsz, ts, c), lambda i: (0, i, OFF_LRU_G // c)),
                  full(conv_w.shape), full(conv_b.shape), full(wa_bd.shape), full(ba.shape),
                  full(wx_bd.shape), full(bx.shape), full(lam.shape)],
        out_specs=pl.BlockSpec((bsz, ts, c), lambda i: (0, i, 0)),
        out_shape=jax.ShapeDtypeStruct((bsz, s, c), BF16),
        scratch_shapes=[pltpu.VMEM((bsz, ts + SUBLANES, c), F32),
                        pltpu.VMEM((c // (2 * LANES), ts * SUBLANES, LANES), F32),
                        pltpu.VMEM((c // (2 * LANES), ts * SUBLANES, LANES), F32),
                        pltpu.VMEM((c // (2 * LANES), SUBLANES, LANES), F32)],
        compiler_params=_cparams(("arbitrary",)),
        name="rg_lru",
    )(z3, z3, conv_w, conv_b, wa_bd, ba, wx_bd, bx, lam)


def _gla_kernel(q_ref, k_ref, v_ref, og_ref, gl_ref, bg_ref, hn_ref, y_ref, st_ref):
    tc = q_ref.shape[1]
    cs = GLA_CHUNK

    @pl.when(pl.program_id(1) == 0)
    def _():
        st_ref[...] = jnp.zeros(st_ref.shape, F32)

    row = lax.broadcasted_iota(jnp.int32, (tc, tc), 0)
    col = lax.broadcasted_iota(jnp.int32, (tc, tc), 1)
    blk = jnp.where(row // cs == col // cs, 1.0, 0.0)
    tri = jnp.where(row >= col, blk, 0.0)
    causal = tri > 0.0
    for h in range(GLA_HEADS):
        dk = slice(h * GLA_DK, (h + 1) * GLA_DK)
        dv = slice(h * GLA_DV, (h + 1) * GLA_DV)
        x = gl_ref[0, :, dk].astype(F32) + bg_ref[:, dk]
        gk = (jnp.minimum(x, 0.0) - jnp.log1p(jnp.exp(-jnp.abs(x)))) * (1.0 / GLA_TAU)
        bcum = jnp.dot(tri, gk, precision=lax.Precision.HIGHEST, preferred_element_type=F32)
        btot = jnp.dot(blk, gk, precision=lax.Precision.HIGHEST, preferred_element_type=F32)
        q = q_ref[0, :, dk].astype(F32) * (GLA_DK ** -0.5)
        k = k_ref[0, :, dk].astype(F32)
        v = v_ref[0, :, dv]
        qs = (q * jnp.exp(bcum)).astype(BF16)
        ks = (k * jnp.exp(-bcum)).astype(BF16)
        ke = (k * jnp.exp(btot - bcum)).astype(BF16)
        att = lax.dot_general(qs, ks, (((1,), (1,)), ((), ())), preferred_element_type=F32)
        att = jnp.where(causal, att, 0.0).astype(BF16)
        o_intra = jnp.dot(att, v, preferred_element_type=F32)
        st = st_ref[h]
        outs = []
        for c in range(tc // cs):
            sl = slice(c * cs, (c + 1) * cs)
            outs.append(o_intra[sl] + lax.dot_general(
                qs[sl], st.astype(BF16), (((1,), (1,)), ((), ())), preferred_element_type=F32))
            ds = lax.dot_general(v[sl], ke[sl], (((0,), (0,)), ((), ())),
                                 preferred_element_type=F32)
            st = jnp.exp(btot[c * cs:c * cs + 1, :]) * st + ds
        st_ref[h] = st
        o = jnp.concatenate(outs, axis=0)
        o = o * lax.rsqrt(jnp.mean(o * o, axis=-1, keepdims=True) + EPS)
        y = (o * hn_ref[:, dv]) * _silu(og_ref[0, :, dv].astype(F32))
        y_ref[0, :, dv] = y.astype(y_ref.dtype)


def _gla(z3, b_gate, head_norm):
    bsz, s, _ = z3.shape
    tc = min(TC_GLA, s)
    qb, kb, vb, ob, gb = (OFF_Q // GLA_QK, OFF_K // GLA_QK, OFF_V // GLA_V, OFF_OG // GLA_V,
                          OFF_LR // GLA_QK)
    return pl.pallas_call(
        _gla_kernel,
        grid=(bsz, s // tc),
        in_specs=[pl.BlockSpec((1, tc, GLA_QK), lambda b, i: (b, i, qb)),
                  pl.BlockSpec((1, tc, GLA_QK), lambda b, i: (b, i, kb)),
                  pl.BlockSpec((1, tc, GLA_V), lambda b, i: (b, i, vb)),
                  pl.BlockSpec((1, tc, GLA_V), lambda b, i: (b, i, ob)),
                  pl.BlockSpec((1, tc, GLA_QK), lambda b, i: (b, i, gb)),
                  pl.BlockSpec((1, GLA_QK), lambda b, i: (0, 0)),
                  pl.BlockSpec((1, GLA_V), lambda b, i: (0, 0))],
        out_specs=pl.BlockSpec((1, tc, GLA_V), lambda b, i: (b, i, 0)),
        out_shape=jax.ShapeDtypeStruct((bsz, s, GLA_V), BF16),
        scratch_shapes=[pltpu.VMEM((GLA_HEADS, GLA_DV, GLA_DK), F32)],
        compiler_params=_cparams(("parallel", "arbitrary")),
        name="gla",
    )(z3, z3, z3, z3, z3, b_gate, head_norm)


def _s5_kernel(u_ref, lre_ref, lim_ref, lst_ref, bre_ref, bim_ref, cre_ref, cim_ref, d_ref,
               y_ref, wbr_ref, wbi_ref, ar_ref, ai_ref, sr_ref, si_ref, hr_ref, hi_ref):
    bsz, ts, cb = u_ref.shape
    rows = bsz * ts
    nst = lre_ref.shape[2]
    npair = nst // (2 * LANES)

    @pl.when(pl.program_id(1) == 0)
    def _():
        lr = jnp.minimum(lre_ref[0], S5_MAX_RE)
        li = lim_ref[0]
        dt = jnp.exp(lst_ref[0])
        mag = jnp.exp(lr * dt)
        ab_re = mag * jnp.cos(li * dt)
        ab_im = mag * jnp.sin(li * dt)
        den = lr * lr + li * li
        coef_re = ((ab_re - 1.0) * lr + ab_im * li) / den
        coef_im = (ab_im * lr - (ab_re - 1.0) * li) / den
        bre = bre_ref[0]
        bim = bim_ref[0]
        wbr_ref[...] = (coef_re * bre - coef_im * bim).astype(BF16)
        wbi_ref[...] = (coef_re * bim + coef_im * bre).astype(BF16)
        for j, v in enumerate(_pair_rows(ab_re, bsz)):
            ar_ref[j] = v
        for j, v in enumerate(_pair_rows(ab_im, bsz)):
            ai_ref[j] = v
        hr_ref[...] = jnp.zeros(hr_ref.shape, F32)
        hi_ref[...] = jnp.zeros(hi_ref.shape, F32)

    u = u_ref[...].reshape(rows, cb)
    _store_paired(sr_ref, jnp.dot(u, wbr_ref[...], preferred_element_type=F32), bsz, ts)
    _store_paired(si_ref, jnp.dot(u, wbi_ref[...], preferred_element_type=F32), bsz, ts)

    ars = [ar_ref[j] for j in range(npair)]
    ais = [ai_ref[j] for j in range(npair)]

    def step(t0, carry):
        hr, hi = list(carry[0]), list(carry[1])
        for k in range(SCAN_UNROLL):
            rs = _step_rows(t0, k)
            for j in range(npair):
                nr = ars[j] * hr[j] - ais[j] * hi[j] + sr_ref[j, rs, :]
                ni = ars[j] * hi[j] + ais[j] * hr[j] + si_ref[j, rs, :]
                hr[j], hi[j] = nr, ni
                sr_ref[j, rs, :] = nr
                si_ref[j, rs, :] = ni
        return tuple(hr), tuple(hi)

    init = (tuple(hr_ref[j] for j in range(npair)), tuple(hi_ref[j] for j in range(npair)))
    hr, hi = lax.fori_loop(0, ts // SCAN_UNROLL, step, init)
    for j in range(npair):
        hr_ref[j] = hr[j]
        hi_ref[j] = hi[j]

    h_re = _load_paired(sr_ref, bsz, ts, nst // LANES).astype(BF16)
    h_im = _load_paired(si_ref, bsz, ts, nst // LANES).astype(BF16)
    y = jnp.dot(h_re, cre_ref[0], preferred_element_type=F32)
    y = y - jnp.dot(h_im, cim_ref[0], preferred_element_type=F32)
    y = y + d_ref[...] * u.astype(F32)
    y_ref[...] = _gelu_tanh(y).astype(y_ref.dtype).reshape(bsz, ts, cb)


def _s5(u3, lre, lim, lst, bre_bd, bim_bd, cre_bd, cim_bd, d_skip):
    bsz, s, c = u3.shape
    assert 2 * bsz == SUBLANES
    ts = min(TS_S5, s)
    cb = S5_CB
    ncb = c // cb
    nst = lre.shape[2]
    rows = bsz * ts
    npair = nst // (2 * LANES)
    blk = lambda shape: pl.BlockSpec((1,) + shape, lambda ci, i: (ci, 0, 0))
    return pl.pallas_call(
        _s5_kernel,
        grid=(ncb, s // ts),
        in_specs=[pl.BlockSpec((bsz, ts, cb), lambda ci, i: (0, i, ci)),
                  blk((1, nst)), blk((1, nst)), blk((1, nst)),
                  blk((cb, nst)), blk((cb, nst)), blk((nst, cb)), blk((nst, cb)),
                  pl.BlockSpec((1, cb), lambda ci, i: (0, ci))],
        out_specs=pl.BlockSpec((bsz, ts, cb), lambda ci, i: (0, i, ci)),
        out_shape=jax.ShapeDtypeStruct((bsz, s, c), BF16),
        scratch_shapes=[pltpu.VMEM((cb, nst), BF16), pltpu.VMEM((cb, nst), BF16),
                        pltpu.VMEM((npair, SUBLANES, LANES), F32),
                        pltpu.VMEM((npair, SUBLANES, LANES), F32),
                        pltpu.VMEM((npair, ts * SUBLANES, LANES), F32),
                        pltpu.VMEM((npair, ts * SUBLANES, LANES), F32),
                        pltpu.VMEM((npair, SUBLANES, LANES), F32),
                        pltpu.VMEM((npair, SUBLANES, LANES), F32)],
        compiler_params=_cparams(("parallel", "arbitrary")),
        name="s5",
    )(u3, lre, lim, lst, bre_bd, bim_bd, cre_bd, cim_bd, d_skip)


ROUTE_GROUP_LANE0 = 0
ROUTE_EXPERT_LANE0 = N_EXPERT_GROUPS
RI_E0, RI_E1, RI_C0, RI_C1, RI_R0, RI_R1 = 0, 1, 2, 3, 4, 5


def _router_kernel(x_ref, g_ref, w_ref, b_ref, ri_ref, rt_ref, cnt_ref, run_ref, wh_ref, wl_ref):
    tm = x_ref.shape[0]
    neg = -jnp.inf

    @pl.when(pl.program_id(0) == 0)
    def _():
        run_ref[...] = jnp.zeros(run_ref.shape, F32)
        w = w_ref[...]
        wh = w.astype(BF16)
        wh_ref[...] = wh
        wl_ref[...] = (w - wh.astype(F32)).astype(BF16)

    xn = _rms(x_ref[...], g_ref[...])
    xh = xn.astype(BF16)
    xl = (xn - xh.astype(F32)).astype(BF16)
    logits = (jnp.dot(xh, wh_ref[...], preferred_element_type=F32)
              + jnp.dot(xl, wh_ref[...], preferred_element_type=F32)
              + jnp.dot(xh, wl_ref[...], preferred_element_type=F32)) + b_ref[...]
    lane = lax.broadcasted_iota(jnp.int32, (tm, LANES), 1)

    def first_lane(mask):
        return jnp.min(jnp.where(mask, lane, LANES), axis=1, keepdims=True)

    lg = jnp.where(lane < N_EXPERT_GROUPS, logits, neg)
    mg = jnp.max(lg, axis=1, keepdims=True)
    sg = jnp.sum(jnp.exp(lg - mg), axis=1, keepdims=True)
    gate_g = 1.0 / sg
    g_idx = first_lane(lg == mg)
    lo = ROUTE_EXPERT_LANE0 + EXPERTS_PER_GROUP * g_idx
    in_group = jnp.abs(2 * (lane - lo) - (EXPERTS_PER_GROUP - 1)) < EXPERTS_PER_GROUP
    le = jnp.where(in_group, logits, neg)
    m1 = jnp.max(le, axis=1, keepdims=True)
    i1 = first_lane(le == m1)
    le2 = jnp.where(lane == i1, neg, le)
    m2 = jnp.max(le2, axis=1, keepdims=True)
    i2 = first_lane(le2 == m2)
    se = jnp.sum(jnp.exp(le - m1), axis=1, keepdims=True)
    p1 = 1.0 / se
    p2 = jnp.exp(m2 - m1) / se
    c0 = gate_g * (p1 / (p1 + p2))
    c1 = gate_g * (p2 / (p1 + p2))

    sel0 = lane == i1
    sel1 = lane == i2
    onehot = jnp.where(sel0, 1.0, jnp.where(sel1, 1.0, 0.0))
    r_i = lax.broadcasted_iota(jnp.int32, (tm, tm), 0)
    c_i = lax.broadcasted_iota(jnp.int32, (tm, tm), 1)
    before = jnp.where(c_i < r_i, 1.0, 0.0).astype(BF16)
    prefix = jnp.dot(before, onehot.astype(BF16), preferred_element_type=F32) + run_ref[...]
    rank0 = jnp.sum(jnp.where(sel0, prefix, 0.0), axis=1, keepdims=True)
    rank1 = jnp.sum(jnp.where(sel1, prefix, 0.0), axis=1, keepdims=True)
    run_ref[...] = run_ref[...] + jnp.sum(onehot, axis=0, keepdims=True)
    cnt_ref[...] = run_ref[...]

    e0 = (i1 - ROUTE_EXPERT_LANE0).astype(F32)
    e1 = (i2 - ROUTE_EXPERT_LANE0).astype(F32)
    rec = jnp.zeros((tm, LANES), F32)
    for ln, val in ((RI_E0, e0), (RI_E1, e1), (RI_C0, c0), (RI_C1, c1), (RI_R0, rank0),
                    (RI_R1, rank1)):
        rec = jnp.where(lane == ln, val, rec)
    ri_ref[...] = rec
    rt_ref[...] = rec.T[0:SUBLANES, :]


def _router(x, g, w_route, b_route):
    t, d = x.shape
    tm = min(TM_ROUTE, t)
    return pl.pallas_call(
        _router_kernel,
        grid=(t // tm,),
        in_specs=[pl.BlockSpec((tm, d), lambda i: (i, 0)),
                  pl.BlockSpec((1, d), lambda i: (0, 0)),
                  pl.BlockSpec((d, LANES), lambda i: (0, 0)),
                  pl.BlockSpec((1, LANES), lambda i: (0, 0))],
        out_specs=[pl.BlockSpec((tm, LANES), lambda i: (i, 0)),
                   pl.BlockSpec((SUBLANES, tm), lambda i: (0, i)),
                   pl.BlockSpec((1, LANES), lambda i: (0, 0))],
        out_shape=[jax.ShapeDtypeStruct((t, LANES), F32),
                   jax.ShapeDtypeStruct((SUBLANES, t), F32),
                   jax.ShapeDtypeStruct((1, LANES), F32)],
        scratch_shapes=[pltpu.VMEM((1, LANES), F32), pltpu.VMEM((d, LANES), BF16),
                        pltpu.VMEM((d, LANES), BF16)],
        compiler_params=_cparams(("arbitrary",)),
        name="router",
    )(x, g, w_route, b_route)


def _piece_groups(n_rows, weights):
    total = sum(weights)
    bounds = [round(n_rows * sum(weights[:k]) / total) for k in range(len(weights) + 1)]
    return [range(bounds[k], bounds[k + 1]) for k in range(len(weights))]


def _dispatch_kernel(n_rows, pos_ref, zrow_ref, x_ref, g_ref, xs_hbm, xbuf, zbuf, sem, zsem):
    i = pl.program_id(0)
    nb = pl.num_programs(0)
    tm, d = x_ref.shape
    t = nb * tm
    tme = zbuf.shape[0]
    slot = i % DISPATCH_BUFS

    def scatter_wait(s):
        for k in range(2):
            pltpu.make_async_copy(xbuf.at[s], xs_hbm.at[pl.ds(0, tm), :], sem.at[s]).wait()

    @pl.when(i == 0)
    def _():
        zbuf[...] = jnp.zeros(zbuf.shape, F32)

        def fill(e):
            row = pl.multiple_of(jnp.maximum(zrow_ref[e], 0), tme)
            return pltpu.make_async_copy(zbuf, xs_hbm.at[pl.ds(row, tme), :], zsem)

        for e in range(n_rows):
            @pl.when(zrow_ref[e] >= 0)
            def _():
                fill(e).start()

        for e in range(n_rows):
            @pl.when(zrow_ref[e] >= 0)
            def _():
                fill(e).wait()

    @pl.when(i >= DISPATCH_BUFS)
    def _():
        scatter_wait(slot)

    xbuf[slot] = _rms(x_ref[...], g_ref[...])
    for r in range(tm):
        for k in range(2):
            dst = pos_ref[k * t + i * tm + r]
            pltpu.make_async_copy(xbuf.at[slot, pl.ds(r, 1), :], xs_hbm.at[pl.ds(dst, 1), :],
                                  sem.at[slot]).start(priority=k)

    @pl.when(i == nb - 1)
    def _():
        for a in range(min(DISPATCH_BUFS, nb)):
            scatter_wait((i - a) % DISPATCH_BUFS)


def _dispatch(x, g, pos_flat, zero_rows, n_slots, tme):
    t, d = x.shape
    tm = min(TM_DISPATCH, t)
    n_e = zero_rows.shape[0]
    return pl.pallas_call(
        functools.partial(_dispatch_kernel, n_e),
        grid_spec=pltpu.PrefetchScalarGridSpec(
            num_scalar_prefetch=2,
            grid=(t // tm,),
            in_specs=[pl.BlockSpec((tm, d), lambda i, *_: (i, 0)),
                      pl.BlockSpec((1, d), lambda i, *_: (0, 0))],
            out_specs=pl.BlockSpec(memory_space=pl.ANY),
            scratch_shapes=[pltpu.VMEM((DISPATCH_BUFS, tm, d), F32),
                            pltpu.VMEM((tme, d), F32),
                            pltpu.SemaphoreType.DMA((DISPATCH_BUFS,)),
                            pltpu.SemaphoreType.DMA(())]),
        out_shape=jax.ShapeDtypeStruct((n_slots, d), F32),
        compiler_params=_cparams(("arbitrary",)),
        name="dispatch",
    )(pos_flat, zero_rows, x, g)


def _experts_kernel(layer, eidx_ref, eseq_ref, meta_ref, x_ref, w1_hbm, w3_hbm, w2_hbm, y_ref,
                    ws1, ws3, ws2, w1b_ref, w3b_ref, w2b_ref, wsem):
    i = pl.program_id(0)
    nv = meta_ref[0]
    ne = meta_ref[1]

    def weight_copies(k):
        e = eseq_ref[k]
        s = k % 2
        return [pltpu.make_async_copy(w_hbm.at[layer, e], ws.at[s], wsem.at[s])
                for w_hbm, ws in ((w1_hbm, ws1), (w3_hbm, ws3), (w2_hbm, ws2))]

    @pl.when(i == 0)
    def _():
        for c in weight_copies(0):
            c.start(priority=WEIGHT_DMA_PRIORITY)

        @pl.when(ne > 1)
        def _():
            for c in weight_copies(1):
                c.start(priority=WEIGHT_DMA_PRIORITY)

    @pl.when(i >= nv)
    def _():
        y_ref[...] = jnp.zeros(y_ref.shape, y_ref.dtype)

    @pl.when(i < nv)
    def _():
        k = eidx_ref[i]

        @pl.when(jnp.logical_or(i == 0, k != eidx_ref[jnp.maximum(i - 1, 0)]))
        def _():
            for c in weight_copies(k):
                c.wait()
            s = k % 2
            w1b_ref[...] = ws1[s].astype(BF16)
            w3b_ref[...] = ws3[s].astype(BF16)
            w2b_ref[...] = ws2[s].astype(BF16)

            @pl.when(k + 2 < ne)
            def _():
                for c in weight_copies(k + 2):
                    c.start(priority=WEIGHT_DMA_PRIORITY)

        xb = x_ref[...].astype(BF16)
        h = _silu(jnp.dot(xb, w1b_ref[...], preferred_element_type=F32))
        h = h * jnp.dot(xb, w3b_ref[...], preferred_element_type=F32)
        y_ref[...] = jnp.dot(h.astype(BF16), w2b_ref[...], preferred_element_type=F32)


def _experts(tile_eidx, expert_seq, meta, xs, w1, w3, w2, layer):
    d, f = w1.shape[-2:]
    nt = tile_eidx.shape[0]
    tme = xs.shape[0] // nt
    hbm = pl.BlockSpec(memory_space=pl.ANY)
    return pl.pallas_call(
        functools.partial(_experts_kernel, layer),
        grid_spec=pltpu.PrefetchScalarGridSpec(
            num_scalar_prefetch=3,
            grid=(nt,),
            in_specs=[pl.BlockSpec((tme, d), lambda i, eidx, eseq, m: (jnp.minimum(i, m[0] - 1), 0)),
                      hbm, hbm, hbm],
            out_specs=pl.BlockSpec((tme, d), lambda i, *_: (i, 0)),
            scratch_shapes=[pltpu.VMEM((2, d, f), F32), pltpu.VMEM((2, d, f), F32),
                            pltpu.VMEM((2, f, d), F32),
                            pltpu.VMEM((d, f), BF16), pltpu.VMEM((d, f), BF16),
                            pltpu.VMEM((f, d), BF16),
                            pltpu.SemaphoreType.DMA((2,))]),
        out_shape=jax.ShapeDtypeStruct((nt * tme, d), F32),
        compiler_params=_cparams(("arbitrary",)),
        name="experts",
    )(tile_eidx, expert_seq, meta, xs, w1, w3, w2)


def _route_plan(rt, counts, tme):
    t = rt.shape[1]
    nt = 2 * t // tme + N_EXPERTS
    e = rt[RI_E0:RI_E1 + 1].astype(jnp.int32)
    rank = rt[RI_R0:RI_R1 + 1].astype(jnp.int32)
    cnt = counts[0, ROUTE_EXPERT_LANE0:ROUTE_EXPERT_LANE0 + N_EXPERTS].astype(jnp.int32)
    padded = ((cnt + tme - 1) // tme) * tme
    ends = jnp.cumsum(padded)
    offs = ends - padded
    experts = jnp.arange(N_EXPERTS, dtype=jnp.int32)[:, None, None]
    pos = jnp.sum(jnp.where(e[None] == experts, offs[:, None, None], 0), axis=0) + rank
    n_valid = (ends[-1] // tme).astype(jnp.int32)
    starts = jnp.arange(nt, dtype=jnp.int32) * tme
    tile_expert = jnp.sum((starts[:, None] >= ends[None, :]).astype(jnp.int32), axis=1)
    last = jnp.take(tile_expert, jnp.maximum(n_valid - 1, 0))
    tile_expert = jnp.where(jnp.arange(nt) < n_valid, tile_expert, last)
    tile_expert = jnp.minimum(tile_expert, N_EXPERTS - 1).astype(jnp.int32)
    used_cum = jnp.cumsum((cnt > 0).astype(jnp.int32))
    n_used = used_cum[-1]
    ordinals = jnp.arange(N_EXPERTS, dtype=jnp.int32)
    expert_seq = jnp.sum((used_cum[None, :] <= ordinals[:, None]).astype(jnp.int32), axis=1)
    expert_seq = jnp.minimum(expert_seq, N_EXPERTS - 1).astype(jnp.int32)
    tile_eidx = (jnp.take(used_cum, tile_expert) - 1).astype(jnp.int32)
    meta = jnp.stack([n_valid, n_used]).astype(jnp.int32)
    pos_flat = pos.reshape(-1)
    tail = jnp.arange(2 * t // tme, nt, dtype=jnp.int32)
    zero_rows = jnp.concatenate([jnp.where(cnt > 0, ends - tme, -1),
                                 jnp.where(tail >= n_valid, tail * tme, -1)]).astype(jnp.int32)
    return tile_eidx, expert_seq, meta, zero_rows, pos_flat, nt


def _ple_kernel(final, pos_ref, ys_hbm, x_ref, ri_ref, g_ref, wg_ref, bg_ref, p_ref, wp_ref,
                gf_ref, o_ref, ybuf, sem):
    i = pl.program_id(0)
    nb = pl.num_programs(0)
    tm, d = o_ref.shape
    t = nb * tm
    slot = i % PLE_ROW_BUFS

    def gather_row(blk, r, s):
        for k in range(2):
            src = pos_ref[k * t + blk * tm + r]
            pltpu.make_async_copy(ys_hbm.at[pl.ds(src, 1), :], ybuf.at[s, k, pl.ds(r, 1), :],
                                  sem.at[s]).start(priority=k)

    def gather_block(blk, s):
        def body(r0, c):
            for k in range(DMA_UNROLL):
                gather_row(blk, r0 * DMA_UNROLL + k, s)
            return c

        lax.fori_loop(0, tm // DMA_UNROLL, body, 0)

    def gather_wait(s):
        for k in range(2):
            pltpu.make_async_copy(ys_hbm.at[pl.ds(0, tm), :], ybuf.at[s, k], sem.at[s]).wait()

    @pl.when(i == 0)
    def _():
        gather_block(0, 0)
        gather_block(jnp.minimum(1, nb - 1), 1)

    gather_wait(slot)
    nxt = jnp.minimum(i + 2, nb - 1)
    other = (i + 2) % PLE_ROW_BUFS
    groups = _piece_groups(tm, [1] * PLE_PIECES)
    ri = ri_ref[...]
    x2 = x_ref[...] + ri[:, RI_C0:RI_C0 + 1] * ybuf[slot, 0] + ri[:, RI_C1:RI_C1 + 1] * ybuf[slot, 1]
    o_ref[...] = x2
    xn = _rms(x2, g_ref[...]).astype(BF16)
    pb = p_ref[0].astype(BF16)
    w = d // PLE_PIECES
    for c in range(PLE_PIECES):
        cs = slice(c * w, (c + 1) * w)
        gate = jax.nn.sigmoid(jnp.dot(xn, wg_ref[0, :, cs], preferred_element_type=F32)
                              + bg_ref[:, cs])
        proj = jnp.dot(pb, wp_ref[0, :, cs], preferred_element_type=F32)
        o_ref[:, cs] = o_ref[:, cs] + gate * proj
        for r in groups[c]:
            gather_row(nxt, r, other)
    if final:
        o_ref[...] = _rms(o_ref[...], gf_ref[...])

    @pl.when(i == nb - 1)
    def _():
        gather_wait((i + 1) % PLE_ROW_BUFS)
        gather_wait(other)


def _ple(x, ys, pos_flat, rinfo, g, wg, bg, p, wp, layer, g_final):
    t, d = x.shape
    tm = min(TM_PLE, t)
    dp = p.shape[-1]
    final = layer == p.shape[0] - 1
    return pl.pallas_call(
        functools.partial(_ple_kernel, final),
        grid_spec=pltpu.PrefetchScalarGridSpec(
            num_scalar_prefetch=1,
            grid=(t // tm,),
            in_specs=[pl.BlockSpec(memory_space=pl.ANY),
                      pl.BlockSpec((tm, d), lambda i, pos: (i, 0)),
                      pl.BlockSpec((tm, LANES), lambda i, pos: (i, 0)),
                      pl.BlockSpec((1, d), lambda i, pos: (0, 0)),
                      pl.BlockSpec((1, d, d), lambda i, pos: (layer, 0, 0)),
                      pl.BlockSpec((1, d), lambda i, pos: (0, 0)),
                      pl.BlockSpec((1, tm, dp), lambda i, pos: (layer, i, 0)),
                      pl.BlockSpec((1, dp, d), lambda i, pos: (layer, 0, 0)),
                      pl.BlockSpec((1, d), lambda i, pos: (0, 0))],
            out_specs=pl.BlockSpec((tm, d), lambda i, pos: (i, 0)),
            scratch_shapes=[pltpu.VMEM((PLE_ROW_BUFS, 2, tm, d), F32),
                            pltpu.SemaphoreType.DMA((PLE_ROW_BUFS,))]),
        out_shape=jax.ShapeDtypeStruct((t, d), F32),
        compiler_params=_cparams(("arbitrary",)),
        name="ple",
    )(pos_flat, ys, x, rinfo, g, wg, bg, p, wp, g_final)


def _moe_ple(x, layer, g_ffn, w_rg, b_rg, w_re, b_re, w1, w3, w2, g_ple, wg, bg, p, wp,
             g_final):
    d = x.shape[1]
    pad = LANES - N_EXPERT_GROUPS - N_EXPERTS
    w_route = jnp.concatenate([w_rg, w_re, jnp.zeros((d, pad), F32)], axis=1)
    b_route = jnp.concatenate([b_rg, b_re, jnp.zeros((pad,), F32)])[None]
    rinfo, rt, counts = _router(x, g_ffn[None], w_route, b_route)
    tile_eidx, expert_seq, meta, zero_rows, pos_flat, nt = _route_plan(rt, counts, TM_EXP)
    xs = _dispatch(x, g_ffn[None], pos_flat, zero_rows, nt * TM_EXP, TM_EXP)
    ys = _experts(tile_eidx, expert_seq, meta, xs, w1, w3, w2, layer)
    return _ple(x, ys, pos_flat, rinfo, g_ple[None], wg, bg[None], p, wp, layer, g_final[None])


def _block_diag(w, nb):
    h, d, _ = w.shape
    w4 = w.reshape(h // nb, nb, d, d)
    out = jnp.einsum('cadk,ab->cadbk', w4, jnp.eye(nb, dtype=w.dtype))
    return out.reshape(h // nb, nb * d, nb * d)


def _even_layer(x, bsz, e, g, w_in, conv_w, conv_b, w_a, b_a, w_x, b_x, lam, w_gate, b_gate,
                head_norm, w_out):
    t = x.shape[0]
    w_gk = _gate_weight(w_in, e, w_gate)
    z = _norm_matmul(x, g[None], w_in, e, OFF_LR, w_gk)
    z3 = z.reshape(bsz, t // bsz, D_EVEN_Z)
    heads_per_blk = 2 * LANES // LRU_HEAD_DIM
    y_lru = _lru(z3, conv_w, conv_b[None], _block_diag(w_a, heads_per_blk).astype(BF16),
                 b_a[None], _block_diag(w_x, heads_per_blk).astype(BF16), b_x[None], lam[None])
    y_gla = _gla(z3, b_gate[None], head_norm[None])
    return _mm2_res(y_lru.reshape(t, D_LRU), y_gla.reshape(t, GLA_V), w_out, e, x)


def _odd_layer(x, bsz, o, g, w_in, lam_re, lam_im, log_step, b_re, b_im, c_re, c_im, d_skip,
               w_glu):
    t = x.shape[0]
    u = _norm_matmul(x, g[None], w_in, o, D_S5, tn=D_S5)
    gpb = S5_CB // S5_GROUP
    ncb = S5_GROUPS // gpb
    nst = gpb * S5_STATE
    eye = jnp.eye(gpb, dtype=F32)
    lay = lambda a: a.reshape(ncb, 1, nst)
    lst = jnp.repeat(log_step, S5_STATE)
    bexp = lambda b: jnp.einsum('cgph,gk->cghkp', b.reshape(ncb, gpb, S5_STATE, S5_GROUP),
                                eye).reshape(ncb, S5_CB, nst)
    cexp = lambda c: jnp.einsum('cgop,gk->cgpko', c.reshape(ncb, gpb, S5_GROUP, S5_STATE),
                                eye).reshape(ncb, nst, S5_CB).astype(BF16)
    y = _s5(u.reshape(bsz, t // bsz, D_S5), lay(lam_re), lay(lam_im), lay(lst),
            bexp(b_re), bexp(b_im), cexp(c_re), cexp(c_im), d_skip[None])
    return _glu_res(y.reshape(t, D_S5), w_glu, o, x)


def kernel(x, p, norm_mix, norm_ffn, norm_ple, norm_final, ev_w_in, lru_conv_w, lru_conv_b, lru_w_a, lru_b_a, lru_w_x, lru_b_x, lru_lambda, gla_w_gate, gla_b_gate, gla_norm, ev_w_out, od_w_in, s5_lambda_re, s5_lambda_im, s5_log_step, s5_b_re, s5_b_im, s5_c_re, s5_c_im, s5_d, od_w_glu, moe_w_router_group, moe_b_router_group, moe_w_router_expert, moe_b_router_expert, moe_w1, moe_w3, moe_w2, ple_w_gate, ple_b_gate, ple_w_proj):
    bsz, s, d = x.shape
    t = bsz * s
    depth = p.shape[0]
    h = x.reshape(t, d)
    p3 = p.reshape(depth, t, p.shape[-1])
    wg_bf = ple_w_gate.astype(BF16)
    wp_bf = ple_w_proj.astype(BF16)
    for l in range(depth):
        if l % 2 == 0:
            e = l // 2
            h = _even_layer(h, bsz, e, norm_mix[l], ev_w_in, lru_conv_w[e], lru_conv_b[e],
                            lru_w_a[e], lru_b_a[e], lru_w_x[e], lru_b_x[e], lru_lambda[e],
                            gla_w_gate[e], gla_b_gate[e], gla_norm[e], ev_w_out)
        else:
            o = l // 2
            h = _odd_layer(h, bsz, o, norm_mix[l], od_w_in, s5_lambda_re[o], s5_lambda_im[o],
                           s5_log_step[o], s5_b_re[o], s5_b_im[o], s5_c_re[o], s5_c_im[o],
                           s5_d[o], od_w_glu)
        h = _moe_ple(h, l, norm_ffn[l], moe_w_router_group[l], moe_b_router_group[l],
                     moe_w_router_expert[l], moe_b_router_expert[l], moe_w1, moe_w3,
                     moe_w2, norm_ple[l], wg_bf, ple_b_gate[l], p3, wp_bf, norm_final)
    return h.reshape(bsz, s, d)
```

```python
import functools
import math

import jax
import jax.numpy as jnp
from jax import lax
from jax.experimental import pallas as pl
from jax.experimental.pallas import tpu as pltpu

F32 = jnp.float32
BF16 = jnp.bfloat16

D_MODEL = 2048
D_LRU = 1024
LRU_HEADS = 16
LRU_HEAD_DIM = D_LRU // LRU_HEADS
CONV_WIDTH = 4
LRU_C = 8.0
GLA_HEADS = 4
GLA_DK = 128
GLA_DV = 256
GLA_QK = GLA_HEADS * GLA_DK
GLA_V = GLA_HEADS * GLA_DV
GLA_RANK = 16
GLA_TAU = 16.0
GLA_CHUNK = 64
OFF_LRU_X = 0
OFF_LRU_G = OFF_LRU_X + D_LRU
OFF_Q = OFF_LRU_G + D_LRU
OFF_K = OFF_Q + GLA_QK
OFF_V = OFF_K + GLA_QK
OFF_OG = OFF_V + GLA_V
OFF_LR = OFF_OG + GLA_V
D_EVEN_Z = OFF_LR + GLA_QK
D_S5 = 1024
S5_GROUP = 16
S5_GROUPS = D_S5 // S5_GROUP
S5_STATE = 64
S5_MAX_RE = -1e-4
N_EXPERT_GROUPS = 4
EXPERTS_PER_GROUP = 8
N_EXPERTS = N_EXPERT_GROUPS * EXPERTS_PER_GROUP
D_EXPERT = 512
D_PLE = 256
EPS = 1e-6

LANES = 128
SUBLANES = 8
VMEM_LIMIT = 56 * 1024 * 1024

TM_MM = 1024
TM_RES = 2048
TN_MM = 512
TS_LRU = 256
TC_GLA = 256
TS_S5 = 256
S5_CB = 256
TM_ROUTE = 512
TM_EXP = 256
TM_PLE = 256
PLE_PIECES = 4
PLE_ROW_BUFS = 3
TM_DISPATCH = 512
DISPATCH_BUFS = 4
WEIGHT_DMA_PRIORITY = 1
SCAN_UNROLL = 8
DMA_UNROLL = 8


def _cparams(sem):
    return pltpu.CompilerParams(dimension_semantics=sem, vmem_limit_bytes=VMEM_LIMIT)


def _rms(x, g):
    return x * lax.rsqrt(jnp.mean(x * x, axis=-1, keepdims=True) + EPS) * g


def _gelu_tanh(x):
    c = math.sqrt(2.0 / math.pi)
    return 0.5 * x * (1.0 + jnp.tanh(c * (x + 0.044715 * (x * x * x))))


def _softplus(x):
    return jnp.maximum(x, 0.0) + jnp.log1p(jnp.exp(-jnp.abs(x)))


def _silu(x):
    return x * jax.nn.sigmoid(x)


def _norm_matmul_kernel(nj, x_ref, g_ref, w_ref, *rest):
    o_ref, xn_ref = rest[-2:]
    j = pl.program_id(1)

    @pl.when(j == 0)
    def _():
        xn_ref[...] = _rms(x_ref[...], g_ref[...]).astype(BF16)

    @pl.when(j < nj)
    def _():
        o_ref[...] = jnp.dot(xn_ref[...], w_ref[...].astype(BF16),
                             preferred_element_type=F32).astype(o_ref.dtype)

    if len(rest) == 3:
        @pl.when(j >= nj)
        def _():
            o_ref[...] = jnp.dot(xn_ref[...], rest[0][...].astype(BF16),
                                 preferred_element_type=F32).astype(o_ref.dtype)


def _norm_matmul(x, g, w, layer, n_main, w_extra=None, tn=TN_MM):
    t, d = x.shape
    tm = min(TM_MM, t)
    nj = n_main // tn
    in_specs = [pl.BlockSpec((tm, d), lambda i, j: (i, 0)),
                pl.BlockSpec((1, d), lambda i, j: (0, 0)),
                pl.BlockSpec((d, tn), lambda i, j: (layer, jnp.minimum(j, nj - 1)))]
    args = [x, g, w.reshape(-1, w.shape[-1])]
    n_extra = 0
    if w_extra is not None:
        n_extra = 1
        in_specs.append(pl.BlockSpec((d, tn), lambda i, j: (0, 0)))
        args.append(w_extra)
    return pl.pallas_call(
        functools.partial(_norm_matmul_kernel, nj),
        grid=(t // tm, nj + n_extra),
        in_specs=in_specs,
        out_specs=pl.BlockSpec((tm, tn), lambda i, j: (i, j)),
        out_shape=jax.ShapeDtypeStruct((t, (nj + n_extra) * tn), BF16),
        scratch_shapes=[pltpu.VMEM((tm, d), BF16)],
        compiler_params=_cparams(("parallel", "arbitrary")),
        name="norm_matmul",
    )(*args)


def _mm2_res_kernel(a1_ref, a2_ref, w_ref, x_ref, o_ref):
    k1 = a1_ref.shape[1]
    acc = jnp.dot(a1_ref[...], w_ref[0, 0:k1, :].astype(BF16), preferred_element_type=F32)
    acc = acc + jnp.dot(a2_ref[...], w_ref[0, k1:, :].astype(BF16), preferred_element_type=F32)
    o_ref[...] = x_ref[...] + acc


def _mm2_res(a1, a2, w, layer, x):
    t, k1 = a1.shape
    k2 = a2.shape[1]
    n = w.shape[2]
    tm, tn = min(TM_RES, t), TN_MM
    return pl.pallas_call(
        _mm2_res_kernel,
        grid=(t // tm, n // tn),
        in_specs=[pl.BlockSpec((tm, k1), lambda i, j: (i, 0)),
                  pl.BlockSpec((tm, k2), lambda i, j: (i, 0)),
                  pl.BlockSpec((1, k1 + k2, tn), lambda i, j: (layer, 0, j)),
                  pl.BlockSpec((tm, tn), lambda i, j: (i, j))],
        out_specs=pl.BlockSpec((tm, tn), lambda i, j: (i, j)),
        out_shape=jax.ShapeDtypeStruct((t, n), F32),
        compiler_params=_cparams(("parallel", "arbitrary")),
        name="out_proj",
    )(a1, a2, w, x)


def _glu_res_kernel(y_ref, w1_ref, w2_ref, x_ref, o_ref):
    y = y_ref[...]
    z1 = jnp.dot(y, w1_ref[0].astype(BF16), preferred_element_type=F32)
    z2 = jnp.dot(y, w2_ref[0].astype(BF16), preferred_element_type=F32)
    o_ref[...] = x_ref[...] + z1 * jax.nn.sigmoid(z2)


def _glu_res(y, w, layer, x):
    t, k = y.shape
    n = w.shape[2] // 2
    tm, tn = min(TM_RES, t), TN_MM
    nj = n // tn
    return pl.pallas_call(
        _glu_res_kernel,
        grid=(t // tm, nj),
        in_specs=[pl.BlockSpec((tm, k), lambda i, j: (i, 0)),
                  pl.BlockSpec((1, k, tn), lambda i, j: (layer, 0, j)),
                  pl.BlockSpec((1, k, tn), lambda i, j: (layer, 0, j + nj)),
                  pl.BlockSpec((tm, tn), lambda i, j: (i, j))],
        out_specs=pl.BlockSpec((tm, tn), lambda i, j: (i, j)),
        out_shape=jax.ShapeDtypeStruct((t, n), F32),
        compiler_params=_cparams(("parallel", "arbitrary")),
        name="glu_proj",
    )(y, w, w, x)


def _gate_weight_kernel(a_ref, b_ref, o_ref):
    o_ref[...] = jnp.dot(a_ref[:, 0:b_ref.shape[0]], b_ref[...],
                         precision=lax.Precision.HIGHEST, preferred_element_type=F32)


def _gate_weight(w_in, layer, w_gate):
    d = w_in.shape[1]
    return pl.pallas_call(
        _gate_weight_kernel,
        grid=(1,),
        in_specs=[pl.BlockSpec((d, LANES), lambda i: (layer, OFF_LR // LANES)),
                  pl.BlockSpec(w_gate.shape, lambda i: (0, 0))],
        out_specs=pl.BlockSpec((d, w_gate.shape[1]), lambda i: (0, 0)),
        out_shape=jax.ShapeDtypeStruct((d, w_gate.shape[1]), F32),
        name="gate_weight",
    )(w_in.reshape(-1, w_in.shape[-1]), w_gate)


def _store_paired(dst_ref, v, bsz, ts):
    for c in range(v.shape[1] // LANES):
        for b in range(bsz):
            dst_ref.at[c // 2][pl.ds((c % 2) * bsz + b, ts, stride=SUBLANES), :] = (
                v[b * ts:(b + 1) * ts, c * LANES:(c + 1) * LANES])


def _load_paired(src_ref, bsz, ts, nchunks):
    cols = []
    for c in range(nchunks):
        cols.append(jnp.concatenate(
            [src_ref.at[c // 2][pl.ds((c % 2) * bsz + b, ts, stride=SUBLANES), :]
             for b in range(bsz)], axis=0))
    return jnp.concatenate(cols, axis=1)


def _step_rows(t0, k):
    base = pl.multiple_of(t0 * (SCAN_UNROLL * SUBLANES), SCAN_UNROLL * SUBLANES)
    return pl.ds(base + k * SUBLANES, SUBLANES)


def _pair_rows(v, batch):
    out = []
    for j in range(v.shape[1] // (2 * LANES)):
        lo = jnp.broadcast_to(v[:, (2 * j) * LANES:(2 * j + 1) * LANES], (batch, LANES))
        hi = jnp.broadcast_to(v[:, (2 * j + 1) * LANES:(2 * j + 2) * LANES], (batch, LANES))
        out.append(jnp.concatenate([lo, hi], axis=0))
    return out


def _lru_kernel(zx_ref, zg_ref, cw_ref, cb_ref, wa_ref, ba_ref, wx_ref, bx_ref, lam_ref,
                y_ref, xe_ref, a_ref, u_ref, h_ref):
    bsz, ts, c = zx_ref.shape
    rows = bsz * ts
    npair = c // (2 * LANES)
    nblk = wa_ref.shape[0]
    wblk = wa_ref.shape[1]

    @pl.when(pl.program_id(0) == 0)
    def _():
        xe_ref[:, 0:SUBLANES, :] = jnp.zeros((bsz, SUBLANES, c), F32)
        h_ref[...] = jnp.zeros(h_ref.shape, F32)

    x = zx_ref[...].astype(F32)
    xe_ref[:, SUBLANES:SUBLANES + ts, :] = x
    xc = cb_ref[...][None]
    for k in range(CONV_WIDTH):
        off = SUBLANES - (CONV_WIDTH - 1) + k
        xc = xc + cw_ref[k:k + 1, :][None] * xe_ref[:, off:off + ts, :]
    xe_ref[:, 0:SUBLANES, :] = x[:, ts - SUBLANES:, :]
    xc = xc.reshape(rows, c)

    xcb = xc.astype(BF16)
    ra, ia = [], []
    for b in range(nblk):
        xs = xcb[:, b * wblk:(b + 1) * wblk]
        ra.append(jnp.dot(xs, wa_ref[b], preferred_element_type=F32))
        ia.append(jnp.dot(xs, wx_ref[b], preferred_element_type=F32))
    r = jax.nn.sigmoid(jnp.concatenate(ra, axis=1) + ba_ref[...])
    i = jax.nn.sigmoid(jnp.concatenate(ia, axis=1) + bx_ref[...])
    log_a = (-LRU_C) * r * _softplus(-lam_ref[...])
    a = jnp.exp(log_a)
    mult = jnp.sqrt(1.0 - a * a)
    u = mult * (i * xc)
    _store_paired(a_ref, a, bsz, ts)
    _store_paired(u_ref, u, bsz, ts)

    def step(t0, hs):
        hs = list(hs)
        for k in range(SCAN_UNROLL):
            rs = _step_rows(t0, k)
            for j in range(npair):
                hs[j] = a_ref[j, rs, :] * hs[j] + u_ref[j, rs, :]
                u_ref[j, rs, :] = hs[j]
        return tuple(hs)

    hs = lax.fori_loop(0, ts // SCAN_UNROLL, step, tuple(h_ref[j] for j in range(npair)))
    for j in range(npair):
        h_ref[j] = hs[j]

    h = _load_paired(u_ref, bsz, ts, c // LANES)
    g = zg_ref[...].astype(F32).reshape(rows, c)
    y_ref[...] = (h * _gelu_tanh(g)).astype(y_ref.dtype).reshape(bsz, ts, c)


def _lru(z3, conv_w, conv_b, wa_bd, ba, wx_bd, bx, lam):
    bsz, s, _ = z3.shape
    assert 2 * bsz == SUBLANES
    ts = min(TS_LRU, s)
    c = D_LRU
    rows = bsz * ts
    full = lambda shape: pl.BlockSpec(shape, lambda i: (0,) * len(shape))
    return pl.pallas_call(
        _lru_kernel,
        grid=(s // ts,),
        in_specs=[pl.BlockSpec((bsz, ts, c), lambda i: (0, i, OFF_LRU_X // c)),
                  pl.BlockSpec((bsz, ts, c), lambda i: (0, i, OFF_LRU_G // c)),
                  full(conv_w.shape), full(conv_b.shape), full(wa_bd.shape), full(ba.shape),
                  full(wx_bd.shape), full(bx.shape), full(lam.shape)],
        out_specs=pl.BlockSpec((bsz, ts, c), lambda i: (0, i, 0)),
        out_shape=jax.ShapeDtypeStruct((bsz, s, c), BF16),
        scratch_shapes=[pltpu.VMEM((bsz, ts + SUBLANES, c), F32),
                        pltpu.VMEM((c // (2 * LANES), ts * SUBLANES, LANES), F32),
                        pltpu.VMEM((c // (2 * LANES), ts * SUBLANES, LANES), F32),
                        pltpu.VMEM((c // (2 * LANES), SUBLANES, LANES), F32)],
        compiler_params=_cparams(("arbitrary",)),
        name="rg_lru",
    )(z3, z3, conv_w, conv_b, wa_bd, ba, wx_bd, bx, lam)


def _gla_kernel(q_ref, k_ref, v_ref, og_ref, gl_ref, bg_ref, hn_ref, y_ref, st_ref):
    tc = q_ref.shape[1]
    cs = GLA_CHUNK

    @pl.when(pl.program_id(1) == 0)
    def _():
        st_ref[...] = jnp.zeros(st_ref.shape, F32)

    row = lax.broadcasted_iota(jnp.int32, (tc, tc), 0)
    col = lax.broadcasted_iota(jnp.int32, (tc, tc), 1)
    blk = jnp.where(row // cs == col // cs, 1.0, 0.0)
    tri = jnp.where(row >= col, blk, 0.0)
    causal = tri > 0.0
    for h in range(GLA_HEADS):
        dk = slice(h * GLA_DK, (h + 1) * GLA_DK)
        dv = slice(h * GLA_DV, (h + 1) * GLA_DV)
        x = gl_ref[0, :, dk].astype(F32) + bg_ref[:, dk]
        gk = (jnp.minimum(x, 0.0) - jnp.log1p(jnp.exp(-jnp.abs(x)))) * (1.0 / GLA_TAU)
        bcum = jnp.dot(tri, gk, precision=lax.Precision.HIGHEST, preferred_element_type=F32)
        btot = jnp.dot(blk, gk, precision=lax.Precision.HIGHEST, preferred_element_type=F32)
        q = q_ref[0, :, dk].astype(F32) * (GLA_DK ** -0.5)
        k = k_ref[0, :, dk].astype(F32)
        v = v_ref[0, :, dv]
        qs = (q * jnp.exp(bcum)).astype(BF16)
        ks = (k * jnp.exp(-bcum)).astype(BF16)
        ke = (k * jnp.exp(btot - bcum)).astype(BF16)
        att = lax.dot_general(qs, ks, (((1,), (1,)), ((), ())), preferred_element_type=F32)
        att = jnp.where(causal, att, 0.0).astype(BF16)
        o_intra = jnp.dot(att, v, preferred_element_type=F32)
        st = st_ref[h]
        outs = []
        for c in range(tc // cs):
            sl = slice(c * cs, (c + 1) * cs)
            outs.append(o_intra[sl] + lax.dot_general(
                qs[sl], st.astype(BF16), (((1,), (1,)), ((), ())), preferred_element_type=F32))
            ds = lax.dot_general(v[sl], ke[sl], (((0,), (0,)), ((), ())),
                                 preferred_element_type=F32)
            st = jnp.exp(btot[c * cs:c * cs + 1, :]) * st + ds
        st_ref[h] = st
        o = jnp.concatenate(outs, axis=0)
        o = o * lax.rsqrt(jnp.mean(o * o, axis=-1, keepdims=True) + EPS)
        y = (o * hn_ref[:, dv]) * _silu(og_ref[0, :, dv].astype(F32))
        y_ref[0, :, dv] = y.astype(y_ref.dtype)


def _gla(z3, b_gate, head_norm):
    bsz, s, _ = z3.shape
    tc = min(TC_GLA, s)
    qb, kb, vb, ob, gb = (OFF_Q // GLA_QK, OFF_K // GLA_QK, OFF_V // GLA_V, OFF_OG // GLA_V,
                          OFF_LR // GLA_QK)
    return pl.pallas_call(
        _gla_kernel,
        grid=(bsz, s // tc),
        in_specs=[pl.BlockSpec((1, tc, GLA_QK), lambda b, i: (b, i, qb)),
                  pl.BlockSpec((1, tc, GLA_QK), lambda b, i: (b, i, kb)),
                  pl.BlockSpec((1, tc, GLA_V), lambda b, i: (b, i, vb)),
                  pl.BlockSpec((1, tc, GLA_V), lambda b, i: (b, i, ob)),
                  pl.BlockSpec((1, tc, GLA_QK), lambda b, i: (b, i, gb)),
                  pl.BlockSpec((1, GLA_QK), lambda b, i: (0, 0)),
                  pl.BlockSpec((1, GLA_V), lambda b, i: (0, 0))],
        out_specs=pl.BlockSpec((1, tc, GLA_V), lambda b, i: (b, i, 0)),
        out_shape=jax.ShapeDtypeStruct((bsz, s, GLA_V), BF16),
        scratch_shapes=[pltpu.VMEM((GLA_HEADS, GLA_DV, GLA_DK), F32)],
        compiler_params=_cparams(("parallel", "arbitrary")),
        name="gla",
    )(z3, z3, z3, z3, z3, b_gate, head_norm)


def _s5_kernel(u_ref, lre_ref, lim_ref, lst_ref, bre_ref, bim_ref, cre_ref, cim_ref, d_ref,
               y_ref, wbr_ref, wbi_ref, ar_ref, ai_ref, sr_ref, si_ref, hr_ref, hi_ref):
    bsz, ts, cb = u_ref.shape
    rows = bsz * ts
    nst = lre_ref.shape[2]
    npair = nst // (2 * LANES)

    @pl.when(pl.program_id(1) == 0)
    def _():
        lr = jnp.minimum(lre_ref[0], S5_MAX_RE)
        li = lim_ref[0]
        dt = jnp.exp(lst_ref[0])
        mag = jnp.exp(lr * dt)
        ab_re = mag * jnp.cos(li * dt)
        ab_im = mag * jnp.sin(li * dt)
        den = lr * lr + li * li
        coef_re = ((ab_re - 1.0) * lr + ab_im * li) / den
        coef_im = (ab_im * lr - (ab_re - 1.0) * li) / den
        bre = bre_ref[0]
        bim = bim_ref[0]
        wbr_ref[...] = (coef_re * bre - coef_im * bim).astype(BF16)
        wbi_ref[...] = (coef_re * bim + coef_im * bre).astype(BF16)
        for j, v in enumerate(_pair_rows(ab_re, bsz)):
            ar_ref[j] = v
        for j, v in enumerate(_pair_rows(ab_im, bsz)):
            ai_ref[j] = v
        hr_ref[...] = jnp.zeros(hr_ref.shape, F32)
        hi_ref[...] = jnp.zeros(hi_ref.shape, F32)

    u = u_ref[...].reshape(rows, cb)
    _store_paired(sr_ref, jnp.dot(u, wbr_ref[...], preferred_element_type=F32), bsz, ts)
    _store_paired(si_ref, jnp.dot(u, wbi_ref[...], preferred_element_type=F32), bsz, ts)

    ars = [ar_ref[j] for j in range(npair)]
    ais = [ai_ref[j] for j in range(npair)]

    def step(t0, carry):
        hr, hi = list(carry[0]), list(carry[1])
        for k in range(SCAN_UNROLL):
            rs = _step_rows(t0, k)
            for j in range(npair):
                nr = ars[j] * hr[j] - ais[j] * hi[j] + sr_ref[j, rs, :]
                ni = ars[j] * hi[j] + ais[j] * hr[j] + si_ref[j, rs, :]
                hr[j], hi[j] = nr, ni
                sr_ref[j, rs, :] = nr
                si_ref[j, rs, :] = ni
        return tuple(hr), tuple(hi)

    init = (tuple(hr_ref[j] for j in range(npair)), tuple(hi_ref[j] for j in range(npair)))
    hr, hi = lax.fori_loop(0, ts // SCAN_UNROLL, step, init)
    for j in range(npair):
        hr_ref[j] = hr[j]
        hi_ref[j] = hi[j]

    h_re = _load_paired(sr_ref, bsz, ts, nst // LANES).astype(BF16)
    h_im = _load_paired(si_ref, bsz, ts, nst // LANES).astype(BF16)
    y = jnp.dot(h_re, cre_ref[0], preferred_element_type=F32)
    y = y - jnp.dot(h_im, cim_ref[0], preferred_element_type=F32)
    y = y + d_ref[...] * u.astype(F32)
    y_ref[...] = _gelu_tanh(y).astype(y_ref.dtype).reshape(bsz, ts, cb)


def _s5(u3, lre, lim, lst, bre_bd, bim_bd, cre_bd, cim_bd, d_skip):
    bsz, s, c = u3.shape
    assert 2 * bsz == SUBLANES
    ts = min(TS_S5, s)
    cb = S5_CB
    ncb = c // cb
    nst = lre.shape[2]
    rows = bsz * ts
    npair = nst // (2 * LANES)
    blk = lambda shape: pl.BlockSpec((1,) + shape, lambda ci, i: (ci, 0, 0))
    return pl.pallas_call(
        _s5_kernel,
        grid=(ncb, s // ts),
        in_specs=[pl.BlockSpec((bsz, ts, cb), lambda ci, i: (0, i, ci)),
                  blk((1, nst)), blk((1, nst)), blk((1, nst)),
                  blk((cb, nst)), blk((cb, nst)), blk((nst, cb)), blk((nst, cb)),
                  pl.BlockSpec((1, cb), lambda ci, i: (0, ci))],
        out_specs=pl.BlockSpec((bsz, ts, cb), lambda ci, i: (0, i, ci)),
        out_shape=jax.ShapeDtypeStruct((bsz, s, c), BF16),
        scratch_shapes=[pltpu.VMEM((cb, nst), BF16), pltpu.VMEM((cb, nst), BF16),
                        pltpu.VMEM((npair, SUBLANES, LANES), F32),
                        pltpu.VMEM((npair, SUBLANES, LANES), F32),
                        pltpu.VMEM((npair, ts * SUBLANES, LANES), F32),
                        pltpu.VMEM((npair, ts * SUBLANES, LANES), F32),
                        pltpu.VMEM((npair, SUBLANES, LANES), F32),
                        pltpu.VMEM((npair, SUBLANES, LANES), F32)],
        compiler_params=_cparams(("parallel", "arbitrary")),
        name="s5",
    )(u3, lre, lim, lst, bre_bd, bim_bd, cre_bd, cim_bd, d_skip)


ROUTE_GROUP_LANE0 = 0
ROUTE_EXPERT_LANE0 = N_EXPERT_GROUPS
RI_E0, RI_E1, RI_C0, RI_C1, RI_R0, RI_R1 = 0, 1, 2, 3, 4, 5


def _router_kernel(x_ref, g_ref, w_ref, b_ref, ri_ref, rt_ref, cnt_ref, run_ref, wh_ref, wl_ref):
    tm = x_ref.shape[0]
    neg = -jnp.inf

    @pl.when(pl.program_id(0) == 0)
    def _():
        run_ref[...] = jnp.zeros(run_ref.shape, F32)
        w = w_ref[...]
        wh = w.astype(BF16)
        wh_ref[...] = wh
        wl_ref[...] = (w - wh.astype(F32)).astype(BF16)

    xn = _rms(x_ref[...], g_ref[...])
    xh = xn.astype(BF16)
    xl = (xn - xh.astype(F32)).astype(BF16)
    logits = (jnp.dot(xh, wh_ref[...], preferred_element_type=F32)
              + jnp.dot(xl, wh_ref[...], preferred_element_type=F32)
              + jnp.dot(xh, wl_ref[...], preferred_element_type=F32)) + b_ref[...]
    lane = lax.broadcasted_iota(jnp.int32, (tm, LANES), 1)

    def first_lane(mask):
        return jnp.min(jnp.where(mask, lane, LANES), axis=1, keepdims=True)

    lg = jnp.where(lane < N_EXPERT_GROUPS, logits, neg)
    mg = jnp.max(lg, axis=1, keepdims=True)
    sg = jnp.sum(jnp.exp(lg - mg), axis=1, keepdims=True)
    gate_g = 1.0 / sg
    g_idx = first_lane(lg == mg)
    lo = ROUTE_EXPERT_LANE0 + EXPERTS_PER_GROUP * g_idx
    in_group = jnp.abs(2 * (lane - lo) - (EXPERTS_PER_GROUP - 1)) < EXPERTS_PER_GROUP
    le = jnp.where(in_group, logits, neg)
    m1 = jnp.max(le, axis=1, keepdims=True)
    i1 = first_lane(le == m1)
    le2 = jnp.where(lane == i1, neg, le)
    m2 = jnp.max(le2, axis=1, keepdims=True)
    i2 = first_lane(le2 == m2)
    se = jnp.sum(jnp.exp(le - m1), axis=1, keepdims=True)
    p1 = 1.0 / se
    p2 = jnp.exp(m2 - m1) / se
    c0 = gate_g * (p1 / (p1 + p2))
    c1 = gate_g * (p2 / (p1 + p2))

    sel0 = lane == i1
    sel1 = lane == i2
    onehot = jnp.where(sel0, 1.0, jnp.where(sel1, 1.0, 0.0))
    r_i = lax.broadcasted_iota(jnp.int32, (tm, tm), 0)
    c_i = lax.broadcasted_iota(jnp.int32, (tm, tm), 1)
    before = jnp.where(c_i < r_i, 1.0, 0.0).astype(BF16)
    prefix = jnp.dot(before, onehot.astype(BF16), preferred_element_type=F32) + run_ref[...]
    rank0 = jnp.sum(jnp.where(sel0, prefix, 0.0), axis=1, keepdims=True)
    rank1 = jnp.sum(jnp.where(sel1, prefix, 0.0), axis=1, keepdims=True)
    run_ref[...] = run_ref[...] + jnp.sum(onehot, axis=0, keepdims=True)
    cnt_ref[...] = run_ref[...]

    e0 = (i1 - ROUTE_EXPERT_LANE0).astype(F32)
    e1 = (i2 - ROUTE_EXPERT_LANE0).astype(F32)
    rec = jnp.zeros((tm, LANES), F32)
    for ln, val in ((RI_E0, e0), (RI_E1, e1), (RI_C0, c0), (RI_C1, c1), (RI_R0, rank0),
                    (RI_R1, rank1)):
        rec = jnp.where(lane == ln, val, rec)
    ri_ref[...] = rec
    rt_ref[...] = rec.T[0:SUBLANES, :]


def _router(x, g, w_route, b_route):
    t, d = x.shape
    tm = min(TM_ROUTE, t)
    return pl.pallas_call(
        _router_kernel,
        grid=(t // tm,),
        in_specs=[pl.BlockSpec((tm, d), lambda i: (i, 0)),
                  pl.BlockSpec((1, d), lambda i: (0, 0)),
                  pl.BlockSpec((d, LANES), lambda i: (0, 0)),
                  pl.BlockSpec((1, LANES), lambda i: (0, 0))],
        out_specs=[pl.BlockSpec((tm, LANES), lambda i: (i, 0)),
                   pl.BlockSpec((SUBLANES, tm), lambda i: (0, i)),
                   pl.BlockSpec((1, LANES), lambda i: (0, 0))],
        out_shape=[jax.ShapeDtypeStruct((t, LANES), F32),
                   jax.ShapeDtypeStruct((SUBLANES, t), F32),
                   jax.ShapeDtypeStruct((1, LANES), F32)],
        scratch_shapes=[pltpu.VMEM((1, LANES), F32), pltpu.VMEM((d, LANES), BF16),
                        pltpu.VMEM((d, LANES), BF16)],
        compiler_params=_cparams(("arbitrary",)),
        name="router",
    )(x, g, w_route, b_route)


def _piece_groups(n_rows, weights):
    total = sum(weights)
    bounds = [round(n_rows * sum(weights[:k]) / total) for k in range(len(weights) + 1)]
    return [range(bounds[k], bounds[k + 1]) for k in range(len(weights))]


def _dispatch_kernel(n_rows, pos_ref, zrow_ref, x_ref, g_ref, xs_hbm, xbuf, zbuf, sem, zsem):
    i = pl.program_id(0)
    nb = pl.num_programs(0)
    tm, d = x_ref.shape
    t = nb * tm
    tme = zbuf.shape[0]
    slot = i % DISPATCH_BUFS

    def scatter_wait(s):
        for k in range(2):
            pltpu.make_async_copy(xbuf.at[s], xs_hbm.at[pl.ds(0, tm), :], sem.at[s]).wait()

    @pl.when(i == 0)
    def _():
        zbuf[...] = jnp.zeros(zbuf.shape, F32)

        def fill(e):
            row = pl.multiple_of(jnp.maximum(zrow_ref[e], 0), tme)
            return pltpu.make_async_copy(zbuf, xs_hbm.at[pl.ds(row, tme), :], zsem)

        for e in range(n_rows):
            @pl.when(zrow_ref[e] >= 0)
            def _():
                fill(e).start()

        for e in range(n_rows):
            @pl.when(zrow_ref[e] >= 0)
            def _():
                fill(e).wait()

    @pl.when(i >= DISPATCH_BUFS)
    def _():
        scatter_wait(slot)

    xbuf[slot] = _rms(x_ref[...], g_ref[...])
    for r in range(tm):
        for k in range(2):
            dst = pos_ref[k * t + i * tm + r]
            pltpu.make_async_copy(xbuf.at[slot, pl.ds(r, 1), :], xs_hbm.at[pl.ds(dst, 1), :],
                                  sem.at[slot]).start(priority=k)

    @pl.when(i == nb - 1)
    def _():
        for a in range(min(DISPATCH_BUFS, nb)):
            scatter_wait((i - a) % DISPATCH_BUFS)


def _dispatch(x, g, pos_flat, zero_rows, n_slots, tme):
    t, d = x.shape
    tm = min(TM_DISPATCH, t)
    n_e = zero_rows.shape[0]
    return pl.pallas_call(
        functools.partial(_dispatch_kernel, n_e),
        grid_spec=pltpu.PrefetchScalarGridSpec(
            num_scalar_prefetch=2,
            grid=(t // tm,),
            in_specs=[pl.BlockSpec((tm, d), lambda i, *_: (i, 0)),
                      pl.BlockSpec((1, d), lambda i, *_: (0, 0))],
            out_specs=pl.BlockSpec(memory_space=pl.ANY),
            scratch_shapes=[pltpu.VMEM((DISPATCH_BUFS, tm, d), F32),
                            pltpu.VMEM((tme, d), F32),
                            pltpu.SemaphoreType.DMA((DISPATCH_BUFS,)),
                            pltpu.SemaphoreType.DMA(())]),
        out_shape=jax.ShapeDtypeStruct((n_slots, d), F32),
        compiler_params=_cparams(("arbitrary",)),
        name="dispatch",
    )(pos_flat, zero_rows, x, g)


def _experts_kernel(layer, eidx_ref, eseq_ref, meta_ref, x_ref, w1_hbm, w3_hbm, w2_hbm, y_ref,
                    ws1, ws3, ws2, w1b_ref, w3b_ref, w2b_ref, wsem):
    i = pl.program_id(0)
    nv = meta_ref[0]
    ne = meta_ref[1]

    def weight_copies(k):
        e = eseq_ref[k]
        s = k % 2
        return [pltpu.make_async_copy(w_hbm.at[layer, e], ws.at[s], wsem.at[s])
                for w_hbm, ws in ((w1_hbm, ws1), (w3_hbm, ws3), (w2_hbm, ws2))]

    @pl.when(i == 0)
    def _():
        for c in weight_copies(0):
            c.start(priority=WEIGHT_DMA_PRIORITY)

        @pl.when(ne > 1)
        def _():
            for c in weight_copies(1):
                c.start(priority=WEIGHT_DMA_PRIORITY)

    @pl.when(i >= nv)
    def _():
        y_ref[...] = jnp.zeros(y_ref.shape, y_ref.dtype)

    @pl.when(i < nv)
    def _():
        k = eidx_ref[i]

        @pl.when(jnp.logical_or(i == 0, k != eidx_ref[jnp.maximum(i - 1, 0)]))
        def _():
            for c in weight_copies(k):
                c.wait()
            for s in range(2):
                @pl.when(k % 2 == s)
                def _():
                    w1b_ref[...] = ws1[s].astype(BF16)
                    w3b_ref[...] = ws3[s].astype(BF16)
                    w2b_ref[...] = ws2[s].astype(BF16)

            @pl.when(k + 2 < ne)
            def _():
                for c in weight_copies(k + 2):
                    c.start(priority=WEIGHT_DMA_PRIORITY)

        xb = x_ref[...].astype(BF16)
        h = _silu(jnp.dot(xb, w1b_ref[...], preferred_element_type=F32))
        h = h * jnp.dot(xb, w3b_ref[...], preferred_element_type=F32)
        y_ref[...] = jnp.dot(h.astype(BF16), w2b_ref[...], preferred_element_type=F32)


def _experts(tile_eidx, expert_seq, meta, xs, w1, w3, w2, layer):
    d, f = w1.shape[-2:]
    nt = tile_eidx.shape[0]
    tme = xs.shape[0] // nt
    hbm = pl.BlockSpec(memory_space=pl.ANY)
    return pl.pallas_call(
        functools.partial(_experts_kernel, layer),
        grid_spec=pltpu.PrefetchScalarGridSpec(
            num_scalar_prefetch=3,
            grid=(nt,),
            in_specs=[pl.BlockSpec((tme, d), lambda i, eidx, eseq, m: (jnp.minimum(i, m[0] - 1), 0)),
                      hbm, hbm, hbm],
            out_specs=pl.BlockSpec((tme, d), lambda i, *_: (i, 0)),
            scratch_shapes=[pltpu.VMEM((2, d, f), F32), pltpu.VMEM((2, d, f), F32),
                            pltpu.VMEM((2, f, d), F32),
                            pltpu.VMEM((d, f), BF16), pltpu.VMEM((d, f), BF16),
                            pltpu.VMEM((f, d), BF16),
                            pltpu.SemaphoreType.DMA((2,))]),
        out_shape=jax.ShapeDtypeStruct((nt * tme, d), F32),
        compiler_params=_cparams(("arbitrary",)),
        name="experts",
    )(tile_eidx, expert_seq, meta, xs, w1, w3, w2)


def _route_plan(rt, counts, tme):
    t = rt.shape[1]
    nt = 2 * t // tme + N_EXPERTS
    e = rt[RI_E0:RI_E1 + 1].astype(jnp.int32)
    rank = rt[RI_R0:RI_R1 + 1].astype(jnp.int32)
    cnt = counts[0, ROUTE_EXPERT_LANE0:ROUTE_EXPERT_LANE0 + N_EXPERTS].astype(jnp.int32)
    padded = ((cnt + tme - 1) // tme) * tme
    ends = jnp.cumsum(padded)
    offs = ends - padded
    experts = jnp.arange(N_EXPERTS, dtype=jnp.int32)[:, None, None]
    pos = jnp.sum(jnp.where(e[None] == experts, offs[:, None, None], 0), axis=0) + rank
    n_valid = (ends[-1] // tme).astype(jnp.int32)
    starts = jnp.arange(nt, dtype=jnp.int32) * tme
    tile_expert = jnp.sum((starts[:, None] >= ends[None, :]).astype(jnp.int32), axis=1)
    last = jnp.take(tile_expert, jnp.maximum(n_valid - 1, 0))
    tile_expert = jnp.where(jnp.arange(nt) < n_valid, tile_expert, last)
    tile_expert = jnp.minimum(tile_expert, N_EXPERTS - 1).astype(jnp.int32)
    used_cum = jnp.cumsum((cnt > 0).astype(jnp.int32))
    n_used = used_cum[-1]
    ordinals = jnp.arange(N_EXPERTS, dtype=jnp.int32)
    expert_seq = jnp.sum((used_cum[None, :] <= ordinals[:, None]).astype(jnp.int32), axis=1)
    expert_seq = jnp.minimum(expert_seq, N_EXPERTS - 1).astype(jnp.int32)
    tile_eidx = (jnp.take(used_cum, tile_expert) - 1).astype(jnp.int32)
    meta = jnp.stack([n_valid, n_used]).astype(jnp.int32)
    pos_flat = pos.reshape(-1)
    tail = jnp.arange(2 * t // tme, nt, dtype=jnp.int32)
    zero_rows = jnp.concatenate([jnp.where(cnt > 0, ends - tme, -1),
                                 jnp.where(tail >= n_valid, tail * tme, -1)]).astype(jnp.int32)
    return tile_eidx, expert_seq, meta, zero_rows, pos_flat, nt


def _ple_kernel(final, pos_ref, ys_hbm, x_ref, ri_ref, g_ref, wg_ref, bg_ref, p_ref, wp_ref,
                gf_ref, o_ref, ybuf, sem):
    i = pl.program_id(0)
    nb = pl.num_programs(0)
    tm, d = o_ref.shape
    t = nb * tm
    slot = i % PLE_ROW_BUFS

    def gather_row(blk, r, s):
        for k in range(2):
            src = pos_ref[k * t + blk * tm + r]
            pltpu.make_async_copy(ys_hbm.at[pl.ds(src, 1), :], ybuf.at[s, k, pl.ds(r, 1), :],
                                  sem.at[s]).start(priority=k)

    def gather_block(blk, s):
        def body(r0, c):
            for k in range(DMA_UNROLL):
                gather_row(blk, r0 * DMA_UNROLL + k, s)
            return c

        lax.fori_loop(0, tm // DMA_UNROLL, body, 0)

    def gather_wait(s):
        for k in range(2):
            pltpu.make_async_copy(ys_hbm.at[pl.ds(0, tm), :], ybuf.at[s, k], sem.at[s]).wait()

    @pl.when(i == 0)
    def _():
        gather_block(0, 0)
        gather_block(jnp.minimum(1, nb - 1), 1)

    gather_wait(slot)
    nxt = jnp.minimum(i + 2, nb - 1)
    other = (i + 2) % PLE_ROW_BUFS
    groups = _piece_groups(tm, [1] * PLE_PIECES)
    ri = ri_ref[...]
    x2 = x_ref[...] + ri[:, RI_C0:RI_C0 + 1] * ybuf[slot, 0] + ri[:, RI_C1:RI_C1 + 1] * ybuf[slot, 1]
    o_ref[...] = x2
    xn = _rms(x2, g_ref[...]).astype(BF16)
    pb = p_ref[0].astype(BF16)
    w = d // PLE_PIECES
    for c in range(PLE_PIECES):
        cs = slice(c * w, (c + 1) * w)
        gate = jax.nn.sigmoid(jnp.dot(xn, wg_ref[0, :, cs], preferred_element_type=F32)
                              + bg_ref[:, cs])
        proj = jnp.dot(pb, wp_ref[0, :, cs], preferred_element_type=F32)
        o_ref[:, cs] = o_ref[:, cs] + gate * proj
        for r in groups[c]:
            gather_row(nxt, r, other)
    if final:
        o_ref[...] = _rms(o_ref[...], gf_ref[...])

    @pl.when(i == nb - 1)
    def _():
        gather_wait((i + 1) % PLE_ROW_BUFS)
        gather_wait(other)


def _ple(x, ys, pos_flat, rinfo, g, wg, bg, p, wp, layer, g_final):
    t, d = x.shape
    tm = min(TM_PLE, t)
    dp = p.shape[-1]
    final = layer == p.shape[0] - 1
    return pl.pallas_call(
        functools.partial(_ple_kernel, final),
        grid_spec=pltpu.PrefetchScalarGridSpec(
            num_scalar_prefetch=1,
            grid=(t // tm,),
            in_specs=[pl.BlockSpec(memory_space=pl.ANY),
                      pl.BlockSpec((tm, d), lambda i, pos: (i, 0)),
                      pl.BlockSpec((tm, LANES), lambda i, pos: (i, 0)),
                      pl.BlockSpec((1, d), lambda i, pos: (0, 0)),
                      pl.BlockSpec((1, d, d), lambda i, pos: (layer, 0, 0)),
                      pl.BlockSpec((1, d), lambda i, pos: (0, 0)),
                      pl.BlockSpec((1, tm, dp), lambda i, pos: (layer, i, 0)),
                      pl.BlockSpec((1, dp, d), lambda i, pos: (layer, 0, 0)),
                      pl.BlockSpec((1, d), lambda i, pos: (0, 0))],
            out_specs=pl.BlockSpec((tm, d), lambda i, pos: (i, 0)),
            scratch_shapes=[pltpu.VMEM((PLE_ROW_BUFS, 2, tm, d), F32),
                            pltpu.SemaphoreType.DMA((PLE_ROW_BUFS,))]),
        out_shape=jax.ShapeDtypeStruct((t, d), F32),
        compiler_params=_cparams(("arbitrary",)),
        name="ple",
    )(pos_flat, ys, x, rinfo, g, wg, bg, p, wp, g_final)


def _moe_ple(x, layer, g_ffn, w_rg, b_rg, w_re, b_re, w1, w3, w2, g_ple, wg, bg, p, wp,
             g_final):
    d = x.shape[1]
    pad = LANES - N_EXPERT_GROUPS - N_EXPERTS
    w_route = jnp.concatenate([w_rg, w_re, jnp.zeros((d, pad), F32)], axis=1)
    b_route = jnp.concatenate([b_rg, b_re, jnp.zeros((pad,), F32)])[None]
    rinfo, rt, counts = _router(x, g_ffn[None], w_route, b_route)
    tile_eidx, expert_seq, meta, zero_rows, pos_flat, nt = _route_plan(rt, counts, TM_EXP)
    xs = _dispatch(x, g_ffn[None], pos_flat, zero_rows, nt * TM_EXP, TM_EXP)
    ys = _experts(tile_eidx, expert_seq, meta, xs, w1, w3, w2, layer)
    return _ple(x, ys, pos_flat, rinfo, g_ple[None], wg, bg[None], p, wp, layer, g_final[None])


def _block_diag(w, nb):
    h, d, _ = w.shape
    w4 = w.reshape(h // nb, nb, d, d)
    out = jnp.einsum('cadk,ab->cadbk', w4, jnp.eye(nb, dtype=w.dtype))
    return out.reshape(h // nb, nb * d, nb * d)


def _even_layer(x, bsz, e, g, w_in, conv_w, conv_b, w_a, b_a, w_x, b_x, lam, w_gate, b_gate,
                head_norm, w_out):
    t = x.shape[0]
    w_gk = _gate_weight(w_in, e, w_gate)
    z = _norm_matmul(x, g[None], w_in, e, OFF_LR, w_gk)
    z3 = z.reshape(bsz, t // bsz, D_EVEN_Z)
    heads_per_blk = 2 * LANES // LRU_HEAD_DIM
    y_lru = _lru(z3, conv_w, conv_b[None], _block_diag(w_a, heads_per_blk).astype(BF16),
                 b_a[None], _block_diag(w_x, heads_per_blk).astype(BF16), b_x[None], lam[None])
    y_gla = _gla(z3, b_gate[None], head_norm[None])
    return _mm2_res(y_lru.reshape(t, D_LRU), y_gla.reshape(t, GLA_V), w_out, e, x)


def _odd_layer(x, bsz, o, g, w_in, lam_re, lam_im, log_step, b_re, b_im, c_re, c_im, d_skip,
               w_glu):
    t = x.shape[0]
    u = _norm_matmul(x, g[None], w_in, o, D_S5, tn=D_S5)
    gpb = S5_CB // S5_GROUP
    ncb = S5_GROUPS // gpb
    nst = gpb * S5_STATE
    eye = jnp.eye(gpb, dtype=F32)
    lay = lambda a: a.reshape(ncb, 1, nst)
    lst = jnp.repeat(log_step, S5_STATE)
    bexp = lambda b: jnp.einsum('cgph,gk->cghkp', b.reshape(ncb, gpb, S5_STATE, S5_GROUP),
                                eye).reshape(ncb, S5_CB, nst)
    cexp = lambda c: jnp.einsum('cgop,gk->cgpko', c.reshape(ncb, gpb, S5_GROUP, S5_STATE),
                                eye).reshape(ncb, nst, S5_CB).astype(BF16)
    y = _s5(u.reshape(bsz, t // bsz, D_S5), lay(lam_re), lay(lam_im), lay(lst),
            bexp(b_re), bexp(b_im), cexp(c_re), cexp(c_im), d_skip[None])
    return _glu_res(y.reshape(t, D_S5), w_glu, o, x)


def kernel(x, p, norm_mix, norm_ffn, norm_ple, norm_final, ev_w_in, lru_conv_w, lru_conv_b, lru_w_a, lru_b_a, lru_w_x, lru_b_x, lru_lambda, gla_w_gate, gla_b_gate, gla_norm, ev_w_out, od_w_in, s5_lambda_re, s5_lambda_im, s5_log_step, s5_b_re, s5_b_im, s5_c_re, s5_c_im, s5_d, od_w_glu, moe_w_router_group, moe_b_router_group, moe_w_router_expert, moe_b_router_expert, moe_w1, moe_w3, moe_w2, ple_w_gate, ple_b_gate, ple_w_proj):
    bsz, s, d = x.shape
    t = bsz * s
    depth = p.shape[0]
    h = x.reshape(t, d)
    p3 = p.reshape(depth, t, p.shape[-1])
    wg_bf = ple_w_gate.astype(BF16)
    wp_bf = ple_w_proj.astype(BF16)
    for l in range(depth):
        if l % 2 == 0:
            e = l // 2
            h = _even_layer(h, bsz, e, norm_mix[l], ev_w_in, lru_conv_w[e], lru_conv_b[e],
                            lru_w_a[e], lru_b_a[e], lru_w_x[e], lru_b_x[e], lru_lambda[e],
                            gla_w_gate[e], gla_b_gate[e], gla_norm[e], ev_w_out)
        else:
            o = l // 2
            h = _odd_layer(h, bsz, o, norm_mix[l], od_w_in, s5_lambda_re[o], s5_lambda_im[o],
                           s5_log_step[o], s5_b_re[o], s5_b_im[o], s5_c_re[o], s5_c_im[o],
                           s5_d[o], od_w_glu)
        h = _moe_ple(h, l, norm_ffn[l], moe_w_router_group[l], moe_b_router_group[l],
                     moe_w_router_expert[l], moe_b_router_expert[l], moe_w1, moe_w3,
                     moe_w2, norm_ple[l], wg_bf, ple_b_gate[l], p3, wp_bf, norm_final)
    return h.reshape(bsz, s, d)
```
